```python
import math
import jax, jax.numpy as jnp
from jax import lax
import numpy as np

D_MODEL = 1024
BATCH = 8
SEQ = 2048
DEPTH = 4
DEC_BATCH = 32
DEC_SEQ = 4
PAST_LEN = 8192
PAGE_SIZE = 128

N_META = 16
GLA_HEADS = 4
GLA_WIDTH = D_MODEL // 2
GLA_DV = GLA_WIDTH // GLA_HEADS
GLA_DK = GLA_DV // 2
GLA_GATE_RANK = 16
GLA_GATE_NORM = 16.0
GLA_CHUNK = 64
DIFF_HEADS = 4
DIFF_WIDTH = D_MODEL - GLA_WIDTH
DIFF_DV = DIFF_WIDTH // DIFF_HEADS
DIFF_HEAD_DIM = DIFF_DV // 2
MIX_WIDTH = GLA_WIDTH + DIFF_WIDTH
QUERY_BLOCK = 128
RMS_EPS = 1e-6
NEG_INF = -1e30
SPLIT_SIZES = (GLA_HEADS * GLA_DK, GLA_HEADS * GLA_DK, GLA_WIDTH, GLA_GATE_RANK, GLA_WIDTH,
               DIFF_HEADS * 2 * DIFF_HEAD_DIM, DIFF_HEADS * 2 * DIFF_HEAD_DIM, DIFF_WIDTH, DIFF_WIDTH)
IN_COLS = sum(SPLIT_SIZES)

kernel_name = "hymba_gla_diffattn_step"


def rmsnorm(x, g):
    xf = x.astype(jnp.float32)
    y = xf * lax.rsqrt(jnp.mean(xf * xf, axis=-1, keepdims=True) + RMS_EPS)
    return (y * g.astype(jnp.float32)).astype(x.dtype)


def lambda_init(layer):
    return 0.8 - 0.6 * math.exp(-0.3 * layer)


def alibi_slopes():
    return 2.0 ** (-8.0 * jnp.arange(1, DIFF_HEADS + 1, dtype=jnp.float32) / DIFF_HEADS)


def diff_lambda(lq1, lk1, lq2, lk2, lam_init):
    f = lambda a: a.astype(jnp.float32)
    return jnp.exp(jnp.sum(f(lq1) * f(lk1))) - jnp.exp(jnp.sum(f(lq2) * f(lk2))) + lam_init


def mixer_inputs(h, norm_g, w_in, gla_w_up, gla_b_gate):
    B, L, _ = h.shape
    u = rmsnorm(h, norm_g)
    proj = u @ w_in
    idx = np.cumsum(SPLIT_SIZES)[:-1].tolist()
    qg, kg, vg, glr, zg, qd, kd, vd, zd = jnp.split(proj, idx, axis=-1)
    gate_logit = (glr @ gla_w_up).astype(jnp.float32) + gla_b_gate.astype(jnp.float32)
    log_alpha = (jax.nn.log_sigmoid(gate_logit) / GLA_GATE_NORM).reshape(B, L, GLA_HEADS, GLA_DK)
    qg = qg.reshape(B, L, GLA_HEADS, GLA_DK) * (GLA_DK ** -0.5)
    kg = kg.reshape(B, L, GLA_HEADS, GLA_DK)
    vg = vg.reshape(B, L, GLA_HEADS, GLA_DV)
    qd = qd.reshape(B, L, DIFF_HEADS, 2, DIFF_HEAD_DIM)
    kd = kd.reshape(B, L, DIFF_HEADS, 2 * DIFF_HEAD_DIM)
    vd = vd.reshape(B, L, DIFF_HEADS, DIFF_DV)
    return qg, kg, vg, log_alpha, zg, qd, kd, vd, zd


def gla_chunked(q, k, v, log_alpha, s0, chunk):
    B, L, H, DK = q.shape
    DV = v.shape[-1]
    n = L // chunk
    f32 = jnp.float32
    q = q.astype(f32).reshape(B, n, chunk, H, DK)
    k = k.astype(f32).reshape(B, n, chunk, H, DK)
    v = v.astype(f32).reshape(B, n, chunk, H, DV)
    b = jnp.cumsum(log_alpha.astype(f32).reshape(B, n, chunk, H, DK), axis=2)
    b_last = b[:, :, -1:]
    q_in = q * jnp.exp(b)
    k_in = k * jnp.exp(-b)
    k_st = k * jnp.exp(b_last - b)
    causal = jnp.tril(jnp.ones((chunk, chunk), dtype=bool))
    att = jnp.where(causal, jnp.einsum('bnihd,bnjhd->bnhij', q_in, k_in), 0.0)
    o_intra = jnp.einsum('bnhij,bnjhv->bnihv', att, v)
    ds = jnp.einsum('bnjhd,bnjhv->nbhdv', k_st, v)
    decay = jnp.moveaxis(jnp.exp(b_last[:, :, 0]), 1, 0)

    def step(s, inp):
        dec, d = inp
        return s * dec[..., None] + d, s

    s_fin, s_prev = lax.scan(step, s0.astype(f32), (decay, ds))
    o_inter = jnp.einsum('bnihd,nbhdv->bnihv', q_in, s_prev)
    o = (o_intra + o_inter).reshape(B, L, H, DV)
    return o, s_fin


def diff_attend(q, k, v, q_pos, k_pos, lam):
    B, K = k.shape[:2]
    k = k.reshape(B, K, DIFF_HEADS, 2, DIFF_HEAD_DIM)
    s = jnp.einsum('bqhcd,bkhcd->bhcqk', q, k).astype(jnp.float32) * (DIFF_HEAD_DIM ** -0.5)
    dist = (q_pos[:, None] - k_pos[None, :]).astype(jnp.float32)
    bias = -alibi_slopes()[:, None, None] * jnp.abs(dist)
    s = jnp.where(k_pos[None, :] <= q_pos[:, None], s + bias[:, None], NEG_INF)
    p = jax.nn.softmax(s, axis=-1)
    a = (p[:, :, 0] - lam * p[:, :, 1]).astype(v.dtype)
    return jnp.einsum('bhqk,bkhv->bqhv', a, v)


def prompt_diff_attention(qd, kd, vd, lam):
    B, L = kd.shape[:2]
    k_pos = jnp.arange(L, dtype=jnp.int32)
    o_meta = diff_attend(qd[:, :N_META], kd[:, :N_META], vd[:, :N_META],
                         k_pos[:N_META], k_pos[:N_META], lam)
    q_real = qd[:, N_META:]
    n_blk = q_real.shape[1] // QUERY_BLOCK

    def block(i):
        start = i * QUERY_BLOCK
        q_blk = lax.dynamic_slice_in_dim(q_real, start, QUERY_BLOCK, axis=1)
        q_pos = N_META + start + jnp.arange(QUERY_BLOCK, dtype=jnp.int32)
        return diff_attend(q_blk, kd, vd, q_pos, k_pos, lam)

    o_real = lax.map(block, jnp.arange(n_blk, dtype=jnp.int32))
    o_real = jnp.moveaxis(o_real, 0, 1).reshape(B, n_blk * QUERY_BLOCK, DIFF_HEADS, DIFF_DV)
    return jnp.concatenate([o_meta, o_real], axis=1)


def mixer_output(h, o_gla, zg, o_diff, zd, gla_norm_g, diff_norm_g, lam_init, w_out):
    B, L = h.shape[:2]
    g = rmsnorm(o_gla, gla_norm_g).astype(h.dtype).reshape(B, L, GLA_WIDTH) * jax.nn.silu(zg)
    d = (rmsnorm(o_diff, diff_norm_g) * (1.0 - lam_init)).astype(h.dtype).reshape(B, L, DIFF_WIDTH) * jax.nn.silu(zd)
    return h + jnp.concatenate([g, d], axis=-1) @ w_out


def setup_inputs(seed: int = 0) -> dict:
    key = jax.random.key(seed)
    ks = jax.random.split(key, 20)
    nrm = jax.random.normal
    n_pages = PAST_LEN // PAGE_SIZE
    n_used = DEC_BATCH * n_pages
    n_pool = n_used + (n_used + 3) // 4
    x_prompt = nrm(ks[0], (BATCH, SEQ, D_MODEL), jnp.float32)
    x_sample = nrm(ks[1], (DEC_BATCH, DEC_SEQ, D_MODEL), jnp.float32)
    cache_k = nrm(ks[2], (DEPTH, n_pool, PAGE_SIZE, DIFF_HEADS, 2 * DIFF_HEAD_DIM), jnp.float32)
    cache_v = nrm(ks[3], (DEPTH, n_pool, PAGE_SIZE, DIFF_HEADS, DIFF_DV), jnp.float32)
    state_gla = nrm(ks[4], (DEPTH, DEC_BATCH, GLA_HEADS, GLA_DK, GLA_DV), jnp.float32)
    page_table = jax.random.permutation(ks[5], n_pool)[:n_used].reshape(DEC_BATCH, n_pages).astype(jnp.int32)
    meta_tokens = nrm(ks[6], (N_META, D_MODEL), jnp.float32)
    norm_g = 1.0 + 0.02 * nrm(ks[7], (DEPTH, D_MODEL), jnp.float32)
    w_in = nrm(ks[8], (DEPTH, D_MODEL, IN_COLS), jnp.float32) * D_MODEL ** -0.5
    gla_w_up = nrm(ks[9], (DEPTH, GLA_GATE_RANK, GLA_HEADS * GLA_DK), jnp.float32) * GLA_GATE_RANK ** -0.5
    gla_b_gate = 0.1 * nrm(ks[10], (DEPTH, GLA_HEADS * GLA_DK), jnp.float32)
    gla_norm_g = 1.0 + 0.02 * nrm(ks[11], (DEPTH, GLA_DV), jnp.float32)
    lambda_q1 = 0.1 * nrm(ks[12], (DEPTH, DIFF_HEAD_DIM), jnp.float32)
    lambda_k1 = 0.1 * nrm(ks[13], (DEPTH, DIFF_HEAD_DIM), jnp.float32)
    lambda_q2 = 0.1 * nrm(ks[14], (DEPTH, DIFF_HEAD_DIM), jnp.float32)
    lambda_k2 = 0.1 * nrm(ks[15], (DEPTH, DIFF_HEAD_DIM), jnp.float32)
    diff_norm_g = 1.0 + 0.02 * nrm(ks[16], (DEPTH, DIFF_DV), jnp.float32)
    w_out = nrm(ks[17], (DEPTH, MIX_WIDTH, D_MODEL), jnp.float32) * MIX_WIDTH ** -0.5
    final_norm_g = 1.0 + 0.02 * nrm(ks[18], (D_MODEL,), jnp.float32)
    return {"x_prompt": x_prompt, "x_sample": x_sample, "cache_k": cache_k, "cache_v": cache_v,
            "state_gla": state_gla, "page_table": page_table, "meta_tokens": meta_tokens,
            "norm_g": norm_g, "w_in": w_in, "gla_w_up": gla_w_up, "gla_b_gate": gla_b_gate,
            "gla_norm_g": gla_norm_g, "lambda_q1": lambda_q1, "lambda_k1": lambda_k1,
            "lambda_q2": lambda_q2, "lambda_k2": lambda_k2, "diff_norm_g": diff_norm_g,
            "w_out": w_out, "final_norm_g": final_norm_g}


def reference(x_prompt, x_sample, cache_k, cache_v, state_gla, page_table, meta_tokens, norm_g, w_in,
              gla_w_up, gla_b_gate, gla_norm_g, lambda_q1, lambda_k1, lambda_q2, lambda_k2,
              diff_norm_g, w_out, final_norm_g):
    B = x_prompt.shape[0]
    DB, T = x_sample.shape[:2]
    n_pages = page_table.shape[1]
    past = n_pages * PAGE_SIZE
    hp = jnp.concatenate([jnp.broadcast_to(meta_tokens[None].astype(x_prompt.dtype), (B, N_META, D_MODEL)),
                          x_prompt], axis=1)
    hs = x_sample
    k_pos_s = jnp.arange(past + T, dtype=jnp.int32)
    q_pos_s = past + jnp.arange(T, dtype=jnp.int32)
    pk, pv, ps, sk, sv, ss = [], [], [], [], [], []
    for l in range(DEPTH):
        lam_init = lambda_init(l)
        lam = diff_lambda(lambda_q1[l], lambda_k1[l], lambda_q2[l], lambda_k2[l], lam_init)
        qg, kg, vg, la, zg, qd, kd, vd, zd = mixer_inputs(hp, norm_g[l], w_in[l], gla_w_up[l], gla_b_gate[l])
        s0 = jnp.zeros((B, GLA_HEADS, GLA_DK, GLA_DV), jnp.float32)
        o_m, s_m = gla_chunked(qg[:, :N_META], kg[:, :N_META], vg[:, :N_META], la[:, :N_META], s0, N_META)
        o_r, s_p = gla_chunked(qg[:, N_META:], kg[:, N_META:], vg[:, N_META:], la[:, N_META:], s_m, GLA_CHUNK)
        o_gla = jnp.concatenate([o_m, o_r], axis=1)
        o_diff = prompt_diff_attention(qd, kd, vd, lam)
        hp = mixer_output(hp, o_gla, zg, o_diff, zd, gla_norm_g[l], diff_norm_g[l], lam_init, w_out[l])
        pk.append(kd)
        pv.append(vd)
        ps.append(s_p.astype(x_prompt.dtype))
        qg, kg, vg, la, zg, qd, kd, vd, zd = mixer_inputs(hs, norm_g[l], w_in[l], gla_w_up[l], gla_b_gate[l])
        o_gla, s_s = gla_chunked(qg, kg, vg, la, state_gla[l], T)
        k_past = cache_k[l][page_table].reshape(DB, past, DIFF_HEADS, 2 * DIFF_HEAD_DIM)
        v_past = cache_v[l][page_table].reshape(DB, past, DIFF_HEADS, DIFF_DV)
        k_all = jnp.concatenate([k_past, kd.astype(k_past.dtype)], axis=1)
        v_all = jnp.concatenate([v_past, vd.astype(v_past.dtype)], axis=1)
        o_diff = diff_attend(qd, k_all, v_all, q_pos_s, k_pos_s, lam)
        hs = mixer_output(hs, o_gla, zg, o_diff, zd, gla_norm_g[l], diff_norm_g[l], lam_init, w_out[l])
        sk.append(kd)
        sv.append(vd)
        ss.append(s_s.astype(state_gla.dtype))
    y_prompt = rmsnorm(hp, final_norm_g)[:, N_META:]
    y_sample = rmsnorm(hs, final_norm_g)
    return (y_prompt, y_sample, jnp.stack(pk), jnp.stack(pv), jnp.stack(ps),
            jnp.stack(sk), jnp.stack(sv), jnp.stack(ss))
```

```python
import functools
import math

import jax
import jax.numpy as jnp
from jax import lax
from jax.experimental import pallas as pl
from jax.experimental.pallas import tpu as pltpu

RMS_EPS = 1e-6
NEG_INF = -1e30
GLA_GATE_NORM = 16.0
GLA_CHUNK = 64
LANES = 128
VMEM_LIMIT_BYTES = 56 * 1024 * 1024
BF16 = jnp.bfloat16
F32 = jnp.float32


def _cparams(*sem):
    return pltpu.CompilerParams(dimension_semantics=sem, vmem_limit_bytes=VMEM_LIMIT_BYTES)


def _dot(a, b):
    return jnp.dot(a, b, preferred_element_type=F32)


def _dot_nt(a, b):
    return lax.dot_general(a, b, (((1,), (1,)), ((), ())), preferred_element_type=F32)


def _dot_tn(a, b):
    return lax.dot_general(a, b, (((0,), (0,)), ((), ())), preferred_element_type=F32)


def _lambda_init(layer):
    return 0.8 - 0.6 * math.exp(-0.3 * layer)


def _lam_from(lamv, lam_init):
    t1 = jnp.sum(lamv[0:1, :] * lamv[1:2, :], axis=-1, keepdims=True)
    t2 = jnp.sum(lamv[2:3, :] * lamv[3:4, :], axis=-1, keepdims=True)
    return jnp.exp(t1) - jnp.exp(t2) + lam_init


def _rms(x):
    return x * lax.rsqrt(jnp.mean(x * x, axis=-1, keepdims=True) + RMS_EPS)


def _in_proj_kernel(h_ref, g_ref, w_ref, wglr_ref, wup_ref, bg_ref,
                    qg_ref, kg_ref, vg_ref, la_ref, zg_ref, qd_ref, kd_ref, vd_ref, zd_ref,
                    *, widths, q_scale):
    u = (_rms(h_ref[...]) * g_ref[...]).astype(BF16)
    outs = (qg_ref, kg_ref, vg_ref, zg_ref, qd_ref, kd_ref, vd_ref, zd_ref)
    lo = 0
    for o_ref, wd in zip(outs, widths):
        y = _dot(u, w_ref[:, lo:lo + wd])
        if o_ref is qg_ref:
            y = y * q_scale
        o_ref[...] = y
        lo += wd
    glr = _dot(u, wglr_ref[...]).astype(BF16)
    gate = _dot(glr, wup_ref[...]) + bg_ref[...]
    log_sig = jnp.minimum(gate, 0.0) - jnp.log1p(jnp.exp(-jnp.abs(gate)))
    la_ref[...] = log_sig / GLA_GATE_NORM


def _in_proj(h, g, w_main, w_glr, w_up, b_gate, *, widths, q_scale, tm):
    n, d = h.shape
    gqk, _, gw, _, dqk, _, dw, _ = widths
    out_w = (gqk, gqk, gw, gqk, gw, dqk, dqk, dw, dw)
    row = lambda i: (i, 0)
    fixed = lambda i: (0, 0)
    return pl.pallas_call(
        functools.partial(_in_proj_kernel, widths=widths, q_scale=q_scale),
        grid=(n // tm,),
        in_specs=[pl.BlockSpec((tm, d), row), pl.BlockSpec((1, d), fixed),
                  pl.BlockSpec(w_main.shape, fixed), pl.BlockSpec(w_glr.shape, fixed),
                  pl.BlockSpec(w_up.shape, fixed), pl.BlockSpec(b_gate.shape, fixed)],
        out_specs=[pl.BlockSpec((tm, w), row) for w in out_w],
        out_shape=[jax.ShapeDtypeStruct((n, w), F32) for w in out_w],
        compiler_params=_cparams("parallel"),
        name="in_proj",
    )(h, g, w_main, w_glr, w_up, b_gate)


def _gla_prompt_kernel(q_ref, k_ref, v_ref, la_ref, o_ref, s_ref, S_scr, *, n_meta, n_chunks, dk, dv):
    C = GLA_CHUNK
    S_scr[...] = jnp.zeros_like(S_scr)
    tril = lax.broadcasted_iota(jnp.int32, (C, C), 0) >= lax.broadcasted_iota(jnp.int32, (C, C), 1)
    tri = tril.astype(F32)
    head_a = lax.broadcasted_iota(jnp.int32, (1, 2 * dk), 1) < dk
    block_diag = ((lax.broadcasted_iota(jnp.int32, (2 * dk, 2 * dv), 0) < dk)
                  == (lax.broadcasted_iota(jnp.int32, (2 * dk, 2 * dv), 1) < dv))
    row_col = lax.broadcasted_iota(jnp.int32, (C, 1), 0)

    def chunk(r0, n_valid):
        la = la_ref[pl.ds(r0, C), :]
        q = q_ref[pl.ds(r0, C), :]
        k = k_ref[pl.ds(r0, C), :]
        v = v_ref[pl.ds(r0, C), :]
        if n_valid < C:
            keep = row_col < n_valid
            la = jnp.where(keep, la, 0.0)
            k = jnp.where(keep, k, 0.0)
            v = jnp.where(keep, v, 0.0)
        b = jnp.dot(tri, la, preferred_element_type=F32, precision=lax.Precision.HIGHEST)
        b_last = b[C - 1:C, :]
        q_in = (q * jnp.exp(b)).astype(BF16)
        k_in = (k * jnp.exp(-b)).astype(BF16)
        k_st = (k * jnp.exp(b_last - b)).astype(BF16)
        vb = v.astype(BF16)
        zero = jnp.zeros_like(q_in)
        att_a = jnp.where(tril, _dot_nt(jnp.where(head_a, q_in, zero), k_in), 0.0).astype(BF16)
        att_b = jnp.where(tril, _dot_nt(jnp.where(head_a, zero, q_in), k_in), 0.0).astype(BF16)
        o_intra = jnp.concatenate([_dot(att_a, vb[:, :dv]), _dot(att_b, vb[:, dv:])], axis=-1)
        S = S_scr[...]
        o = o_intra + _dot(q_in, S.astype(BF16))
        la_sq = jnp.concatenate([la, jnp.zeros((2 * dk - C, 2 * dk), F32)], axis=0)
        decay = jnp.exp(jnp.sum(la_sq.T, axis=-1, keepdims=True))
        ds = jnp.where(block_diag, _dot_tn(k_st, vb), 0.0)
        S_scr[...] = S * decay + ds
        return o

    o0 = chunk(0, n_meta)
    o_ref[0:n_meta, :] = o0[0:n_meta, :]

    def body(c, carry):
        r0 = pl.multiple_of(n_meta + c * C, 8)
        o_ref[pl.ds(r0, C), :] = chunk(r0, C)
        return carry

    lax.fori_loop(0, n_chunks, body, 0)
    S = S_scr[...]
    s_ref[0] = S[0:dk, 0:dv]
    s_ref[1] = S[dk:2 * dk, dv:2 * dv]


def _gla_prompt(qg, kg, vg, la, *, batch, seq_len, n_meta, heads, dk, dv):
    n_chunks = (seq_len - n_meta) // GLA_CHUNK
    qk_spec = pl.BlockSpec((seq_len, 2 * dk), lambda b, p: (b, p))
    v_spec = pl.BlockSpec((seq_len, 2 * dv), lambda b, p: (b, p))
    return pl.pallas_call(
        functools.partial(_gla_prompt_kernel, n_meta=n_meta, n_chunks=n_chunks, dk=dk, dv=dv),
        grid=(batch, heads // 2),
        in_specs=[qk_spec, qk_spec, v_spec, qk_spec],
        out_specs=[v_spec, pl.BlockSpec((None, 2, dk, dv), lambda b, p: (b, p, 0, 0))],
        out_shape=[jax.ShapeDtypeStruct((batch * seq_len, heads * dv), F32),
                   jax.ShapeDtypeStruct((batch, heads, dk, dv), F32)],
        scratch_shapes=[pltpu.VMEM((2 * dk, 2 * dv), F32)],
        compiler_params=_cparams("parallel", "parallel"),
        name="gla_prompt",
    )(qg, kg, vg, la)


def _gla_decode_kernel(qT_ref, kT_ref, laT_ref, v_ref, s0_ref, o_ref, s_ref, *, heads, dk, dv, steps):
    for h in range(heads):
        S = s0_ref[h]
        rows = slice(h * dk, (h + 1) * dk)
        cols = slice(h * dv, (h + 1) * dv)
        for t in range(steps):
            S = S * jnp.exp(laT_ref[rows, t:t + 1]) + kT_ref[rows, t:t + 1] * v_ref[t:t + 1, cols]
            o_ref[t:t + 1, cols] = jnp.sum(qT_ref[rows, t:t + 1] * S, axis=0, keepdims=True)
        s_ref[h] = S


def _gla_decode(qT, kT, laT, v, state, layer, *, heads, dk, dv):
    db, _, steps = qT.shape
    t_spec = pl.BlockSpec((None, heads * dk, steps), lambda b: (b, 0, 0))
    v_spec = pl.BlockSpec((None, steps, heads * dv), lambda b: (b, 0, 0))
    return pl.pallas_call(
        functools.partial(_gla_decode_kernel, heads=heads, dk=dk, dv=dv, steps=steps),
        grid=(db,),
        in_specs=[t_spec, t_spec, t_spec, v_spec,
                  pl.BlockSpec((None, None, heads, dk, dv), lambda b: (layer, b, 0, 0, 0))],
        out_specs=[v_spec, pl.BlockSpec((None, heads, dk, dv), lambda b: (b, 0, 0, 0))],
        out_shape=[jax.ShapeDtypeStruct((db, steps, heads * dv), F32),
                   jax.ShapeDtypeStruct((db, heads, dk, dv), F32)],
        compiler_params=_cparams("parallel"),
        name="gla_decode",
    )(qT, kT, laT, v, state)


def _attn_prompt_kernel(slopes_ref, lamv_ref, q_ref, k_ref, v_ref, o_ref, kb_scr, vb_scr, m_scr, acc_scr,
                        *, n_meta, n_q, tq, dh, lam_init):
    TK = LANES
    dv = v_ref.shape[1]
    slope = slopes_ref[pl.program_id(1)]
    lam = _lam_from(lamv_ref[...], lam_init)
    kb_scr[...] = k_ref[...].astype(BF16)
    vb_scr[:, 0:dv] = v_ref[...].astype(BF16)
    vb_scr[:, dv:2 * dv] = jnp.ones((vb_scr.shape[0], dv), BF16)
    first = lax.broadcasted_iota(jnp.int32, (1, LANES), 1) < dh
    k_lane = lax.broadcasted_iota(jnp.int32, (1, TK), 1)
    scale = dh ** -0.5

    def attend(q0, rows, n_full):
        q = (q_ref[pl.ds(q0, rows), :] * scale).astype(BF16)
        zero = jnp.zeros_like(q)
        qm = jnp.concatenate([jnp.where(first, q, zero), jnp.where(first, zero, q)], axis=0)
        r2 = 2 * rows
        q_pos = q0 + lax.rem(lax.broadcasted_iota(jnp.int32, (r2, 1), 0), rows)
        m_scr[0:r2, :] = jnp.full((r2, LANES), NEG_INF, F32)
        acc_scr[0:r2, :] = jnp.zeros((r2, 2 * dv), F32)

        def step(r0, visible):
            kt = kb_scr[pl.ds(r0, TK), :]
            vt = vb_scr[pl.ds(r0, TK), :]
            k_pos = r0 + k_lane
            s = _dot_nt(qm, kt) + slope * (k_pos - q0).astype(F32)
            if visible is not None:
                s = jnp.where(visible(k_pos), s, NEG_INF)
            m_prev = m_scr[0:r2, :]
            m_new = jnp.maximum(m_prev, jnp.max(s, axis=-1, keepdims=True))
            alpha = jnp.exp(m_prev - m_new)
            p = jnp.exp(s - m_new).astype(BF16)
            acc_scr[0:r2, :] = acc_scr[0:r2, :] * jnp.concatenate([alpha, alpha], axis=-1) + _dot(p, vt)
            m_scr[0:r2, :] = m_new

        if n_full is None:
            step(0, lambda k_pos: k_pos <= q_pos)
        else:
            step(0, lambda k_pos: k_pos < n_meta)

            def body(j, carry):
                step(pl.multiple_of(n_meta + j * TK, 16), None)
                return carry

            lax.fori_loop(0, n_full, body, 0)
            step(q0, lambda k_pos: k_pos <= q_pos)
        acc = acc_scr[0:r2, :]
        o = acc[:, 0:dv] / acc[:, dv:2 * dv]
        o_ref[pl.ds(q0, rows), :] = o[0:rows, :] - lam * o[rows:r2, :]

    attend(0, n_meta, None)

    def q_body(i, carry):
        attend(pl.multiple_of(n_meta + i * tq, 16), tq, i)
        return carry

    lax.fori_loop(0, n_q, q_body, 0)


def _attn_prompt(slopes, lamv, qd, kd, vd, *, batch, seq_len, n_meta, heads, dh, dv, lam_init):
    tq = LANES
    n_q = (seq_len - n_meta) // tq
    spec = lambda w: pl.BlockSpec((seq_len, w), lambda b, h: (b, h))
    return pl.pallas_call(
        functools.partial(_attn_prompt_kernel, n_meta=n_meta, n_q=n_q, tq=tq, dh=dh, lam_init=lam_init),
        grid=(batch, heads),
        in_specs=[pl.BlockSpec(memory_space=pltpu.SMEM), pl.BlockSpec(lamv.shape, lambda b, h: (0, 0)),
                  spec(2 * dh), spec(2 * dh), spec(dv)],
        out_specs=spec(dv),
        out_shape=jax.ShapeDtypeStruct((batch * seq_len, heads * dv), F32),
        scratch_shapes=[pltpu.VMEM((seq_len, 2 * dh), BF16), pltpu.VMEM((seq_len, 2 * dv), BF16),
                        pltpu.VMEM((2 * tq, LANES), F32), pltpu.VMEM((2 * tq, 2 * dv), F32)],
        compiler_params=_cparams("parallel", "parallel"),
        name="attn_prompt",
    )(slopes, lamv, qd, kd, vd)


def _attn_decode_kernel(pt_ref, qm_ref, base_ref, slope_ref, newbase_ref, kn_ref, vn_ref, lamv_ref, *rest,
                        pages_per_step, page, past, lam_init):
    del pt_ref
    k_refs = rest[:pages_per_step]
    v_refs = rest[pages_per_step:2 * pages_per_step]
    o_ref, m_scr, l_scr, acc_scr = rest[2 * pages_per_step:]
    j = pl.program_id(1)
    half = o_ref.shape[0]

    @pl.when(j == 0)
    def _():
        m_scr[...] = jnp.full_like(m_scr, NEG_INF)
        l_scr[...] = jnp.zeros_like(l_scr)
        acc_scr[...] = jnp.zeros_like(acc_scr)

    qm = qm_ref[...]
    slope = slope_ref[...]

    def update(s, vals):
        m_prev = m_scr[...]
        m_new = jnp.maximum(m_prev, jnp.max(s, axis=-1, keepdims=True))
        alpha = jnp.exp(m_prev - m_new)
        p = jnp.exp(s - m_new[:, 0:1])
        l_scr[...] = alpha * l_scr[...] + jnp.sum(p, axis=-1, keepdims=True)
        acc_scr[...] = alpha * acc_scr[...] + _dot(p.astype(BF16), vals)
        m_scr[...] = m_new

    for i in range(pages_per_step):
        pos0 = ((j * pages_per_step + i) * page - past).astype(F32)
        s = _dot_nt(qm, k_refs[i][...].astype(BF16)) + (base_ref[...] + slope[:, 0:1] * pos0)
        update(s, v_refs[i][...].astype(BF16))

    @pl.when(j == pl.num_programs(1) - 1)
    def _():
        update(_dot_nt(qm, kn_ref[...].astype(BF16)) + newbase_ref[...], vn_ref[...].astype(BF16))
        lam = _lam_from(lamv_ref[...], lam_init)
        o = acc_scr[...] / l_scr[...]
        o_ref[...] = o[0:half, :] - lam * o[half:2 * half, :]


def _attn_decode(page_table, qm, base, slope_rows, newbase, k_new, v_new, lamv, cache_k, cache_v, layer,
                 *, pages_per_step, page, lam_init):
    db, n_pages = page_table.shape
    rows = qm.shape[1]
    page_rows, width = cache_k.shape[2:]
    fixed = lambda a: pl.BlockSpec(a.shape, lambda b, j, pt: (0,) * a.ndim)
    per_seq = lambda a: pl.BlockSpec((None,) + a.shape[1:], lambda b, j, pt: (b, 0, 0))

    def page_spec(i):
        return pl.BlockSpec((None, None, page_rows, width),
                            lambda b, j, pt: (layer, pt[b, j * pages_per_step + i], 0, 0))

    pages = [page_spec(i) for i in range(pages_per_step)]
    grid_spec = pltpu.PrefetchScalarGridSpec(
        num_scalar_prefetch=1,
        grid=(db, n_pages // pages_per_step),
        in_specs=[per_seq(qm), fixed(base), fixed(slope_rows), fixed(newbase),
                  per_seq(k_new), per_seq(v_new), fixed(lamv)] + pages + pages,
        out_specs=pl.BlockSpec((None, rows // 2, width), lambda b, j, pt: (b, 0, 0)),
        scratch_shapes=[pltpu.VMEM((rows, LANES), F32), pltpu.VMEM((rows, LANES), F32),
                        pltpu.VMEM((rows, width), F32)],
    )
    return pl.pallas_call(
        functools.partial(_attn_decode_kernel, pages_per_step=pages_per_step, page=page,
                          past=n_pages * page, lam_init=lam_init),
        grid_spec=grid_spec,
        out_shape=jax.ShapeDtypeStruct((db, rows // 2, width), F32),
        compiler_params=_cparams("parallel", "arbitrary"),
        name="attn_decode",
    )(page_table, qm, base, slope_rows, newbase, k_new, v_new, lamv,
      *([cache_k] * pages_per_step), *([cache_v] * pages_per_step))


def _mix_out_kernel(h_ref, og_ref, zg_ref, od_ref, zd_ref, gg_ref, gd_ref, w_ref, o_ref,
                    *, heads_g, heads_d, dv_g, dv_d, diff_scale):
    parts = []
    for h in range(heads_g):
        c = slice(h * dv_g, (h + 1) * dv_g)
        z = zg_ref[:, c]
        parts.append((_rms(og_ref[:, c]) * gg_ref[...]) * (z * jax.nn.sigmoid(z)))
    for h in range(heads_d):
        c = slice(h * dv_d, (h + 1) * dv_d)
        z = zd_ref[:, c]
        parts.append(((_rms(od_ref[:, c]) * gd_ref[...]) * diff_scale) * (z * jax.nn.sigmoid(z)))
    mix = jnp.concatenate(parts, axis=-1).astype(BF16)
    o_ref[...] = h_ref[...] + _dot(mix, w_ref[...])


def _mix_out(h, o_gla, zg, o_diff, zd, g_gla, g_diff, w_out, *, dv_g, dv_d, diff_scale, tm):
    n, d = h.shape
    row = lambda i: (i, 0)
    fixed = lambda i: (0, 0)
    blk = lambda a: pl.BlockSpec((tm, a.shape[1]), row)
    return pl.pallas_call(
        functools.partial(_mix_out_kernel, heads_g=o_gla.shape[1] // dv_g, heads_d=o_diff.shape[1] // dv_d,
                          dv_g=dv_g, dv_d=dv_d, diff_scale=diff_scale),
        grid=(n // tm,),
        in_specs=[blk(h), blk(o_gla), blk(zg), blk(o_diff), blk(zd),
                  pl.BlockSpec(g_gla.shape, fixed), pl.BlockSpec(g_diff.shape, fixed),
                  pl.BlockSpec(w_out.shape, fixed)],
        out_specs=blk(h),
        out_shape=jax.ShapeDtypeStruct((n, d), F32),
        compiler_params=_cparams("parallel"),
        name="mix_out",
    )(h, o_gla, zg, o_diff, zd, g_gla, g_diff, w_out)


def _final_norm_kernel(h_ref, g_ref, o_ref, *, skip, chunk):
    def body(i, carry):
        x = h_ref[pl.ds(pl.multiple_of(skip + i * chunk, 8), chunk), :]
        o_ref[pl.ds(pl.multiple_of(i * chunk, 8), chunk), :] = _rms(x) * g_ref[...]
        return carry

    lax.fori_loop(0, o_ref.shape[0] // chunk, body, 0)


def _final_norm(h3, g, *, skip, chunk):
    b, l, d = h3.shape
    return pl.pallas_call(
        functools.partial(_final_norm_kernel, skip=skip, chunk=chunk),
        grid=(b,),
        in_specs=[pl.BlockSpec((None, l, d), lambda i: (i, 0, 0)), pl.BlockSpec(g.shape, lambda i: (0, 0))],
        out_specs=pl.BlockSpec((None, l - skip, d), lambda i: (i, 0, 0)),
        out_shape=jax.ShapeDtypeStruct((b, l - skip, d), F32),
        compiler_params=_cparams("parallel"),
        name="final_norm",
    )(h3, g)


def _row_tile(n, cap):
    best = 8
    for t in range(8, cap + 1, 8):
        if n % t == 0:
            best = t
    return best


def kernel(x_prompt, x_sample, cache_k, cache_v, state_gla, page_table, meta_tokens, norm_g, w_in, gla_w_up, gla_b_gate, gla_norm_g, lambda_q1, lambda_k1, lambda_q2, lambda_k2, diff_norm_g, w_out, final_norm_g):
    B, SEQ, D = x_prompt.shape
    DB, T, _ = x_sample.shape
    DEPTH, NPOOL, PAGE, HD, DK2 = cache_k.shape
    DV = cache_v.shape[-1]
    DH = DK2 // 2
    NM = meta_tokens.shape[0]
    HG, GDK, GDV = state_gla.shape[2:]
    RANK = gla_w_up.shape[1]
    L = NM + SEQ
    NP, NS = B * L, DB * T
    GQK, GW, DQK, DW = HG * GDK, HG * GDV, HD * DK2, HD * DV
    widths = (GQK, GQK, GW, GW, DQK, DQK, DW, DW)
    assert w_in.shape[-1] == sum(widths) + RANK and DK2 == LANES and DV == LANES and 2 * GDK == LANES

    off = 2 * GQK + GW
    w_main = jnp.concatenate([w_in[..., :off], w_in[..., off + RANK:]], axis=-1).astype(BF16)
    w_glr = jnp.pad(w_in[..., off:off + RANK], ((0, 0), (0, 0), (0, LANES - RANK))).astype(BF16)
    w_up = jnp.pad(gla_w_up, ((0, 0), (0, LANES - RANK), (0, 0))).astype(BF16)
    w_out_b = w_out.astype(BF16)
    lamv = jnp.stack([lambda_q1, lambda_k1, lambda_q2, lambda_k2], axis=1)
    slopes = 2.0 ** (-8.0 * jnp.arange(1, HD + 1, dtype=F32) / HD)
    cache_k2 = cache_k.reshape(DEPTH, NPOOL, PAGE * HD, DK2)
    cache_v2 = cache_v.reshape(DEPTH, NPOOL, PAGE * HD, DV)

    rows = 2 * HD * T
    row_h = (jnp.arange(rows) // T) % HD
    row_t = jnp.arange(rows) % T
    slope_rows = jnp.broadcast_to(slopes[row_h][:, None], (rows, LANES))
    col = jnp.arange(PAGE * HD)
    base = jnp.where((col % HD)[None, :] == row_h[:, None],
                     slopes[row_h][:, None] * (col // HD)[None, :].astype(F32), NEG_INF)
    ncol = jnp.arange(LANES)
    new_ok = (ncol[None, :] < T * HD) & ((ncol % HD)[None, :] == row_h[:, None]) & ((ncol // HD)[None, :] <= row_t[:, None])
    newbase = jnp.where(new_ok, slopes[row_h][:, None] * (ncol // HD)[None, :].astype(F32), NEG_INF)

    hp = jnp.concatenate([jnp.broadcast_to(meta_tokens[None], (B, NM, D)), x_prompt], axis=1).reshape(NP, D)
    hs = x_sample.reshape(NS, D)
    tm_p = _row_tile(NP, 512)
    n_pages = page_table.shape[1]
    pages_per_step = max(p for p in range(1, 9) if n_pages % p == 0)
    pk, pv, ps, sk, sv, ss = [], [], [], [], [], []
    for l in range(DEPTH):
        lam_init = _lambda_init(l)
        proj_args = (norm_g[l][None], w_main[l], w_glr[l], w_up[l], gla_b_gate[l][None])
        qg, kg, vg, la, zg, qd, kd, vd, zd = _in_proj(hp, *proj_args, widths=widths, q_scale=GDK ** -0.5, tm=tm_p)
        o_gla, s_p = _gla_prompt(qg, kg, vg, la, batch=B, seq_len=L, n_meta=NM, heads=HG, dk=GDK, dv=GDV)
        o_diff = _attn_prompt(slopes, lamv[l], qd, kd, vd, batch=B, seq_len=L, n_meta=NM, heads=HD,
                              dh=DH, dv=DV, lam_init=lam_init)
        hp = _mix_out(hp, o_gla, zg, o_diff, zd, gla_norm_g[l][None], diff_norm_g[l][None], w_out_b[l],
                      dv_g=GDV, dv_d=DV, diff_scale=1.0 - lam_init, tm=tm_p)
        pk.append(kd.reshape(B, L, HD, DK2))
        pv.append(vd.reshape(B, L, HD, DV))
        ps.append(s_p)
        qg, kg, vg, la, zg, qd, kd, vd, zd = _in_proj(hs, *proj_args, widths=widths, q_scale=GDK ** -0.5, tm=NS)
        tr = lambda a: a.reshape(DB, T, -1).transpose(0, 2, 1)
        o_gla, s_s = _gla_decode(tr(qg), tr(kg), tr(la), vg.reshape(DB, T, GW), state_gla, l,
                                 heads=HG, dk=GDK, dv=GDV)
        q5 = qd.reshape(DB, T, HD, 2, DH) * (DH ** -0.5)
        zeros = jnp.zeros_like(q5[..., 0, :])
        qm = jnp.stack([jnp.concatenate([q5[..., 0, :], zeros], -1), jnp.concatenate([zeros, q5[..., 1, :]], -1)], 1)
        qm = qm.transpose(0, 1, 3, 2, 4).reshape(DB, rows, DK2).astype(BF16)
        pad_new = lambda a: jnp.pad(a.reshape(DB, T * HD, LANES), ((0, 0), (0, LANES - T * HD), (0, 0)))
        o_dec = _attn_decode(page_table, qm, base, slope_rows, newbase, pad_new(kd), pad_new(vd), lamv[l],
                             cache_k2, cache_v2, l, pages_per_step=pages_per_step, page=PAGE, lam_init=lam_init)
        o_diff = o_dec.reshape(DB, HD, T, DV).transpose(0, 2, 1, 3).reshape(NS, DW)
        hs = _mix_out(hs, o_gla.reshape(NS, GW), zg, o_diff, zd, gla_norm_g[l][None], diff_norm_g[l][None],
                      w_out_b[l], dv_g=GDV, dv_d=DV, diff_scale=1.0 - lam_init, tm=NS)
        sk.append(kd.reshape(DB, T, HD, DK2))
        sv.append(vd.reshape(DB, T, HD, DV))
        ss.append(s_s)
    y_prompt = _final_norm(hp.reshape(B, L, D), final_norm_g[None], skip=NM, chunk=LANES)
    y_sample = _final_norm(hs.reshape(1, NS, D), final_norm_g[None], skip=0, chunk=NS).reshape(DB, T, D)
    return (y_prompt, y_sample, jnp.stack(pk), jnp.stack(pv), jnp.stack(ps),
            jnp.stack(sk), jnp.stack(sv), jnp.stack(ss))
```

```python
import functools
import math

import jax
import jax.numpy as jnp
from jax import lax
from jax.experimental import pallas as pl
from jax.experimental.pallas import tpu as pltpu

RMS_EPS = 1e-6
NEG_INF = -1e30
GLA_GATE_NORM = 16.0
GLA_CHUNK = 64
LOG2E = 1.4426950408889634
LANES = 128
VMEM_LIMIT_BYTES = 56 * 1024 * 1024
BF16 = jnp.bfloat16
F32 = jnp.float32


def _cparams(*sem):
    return pltpu.CompilerParams(dimension_semantics=sem, vmem_limit_bytes=VMEM_LIMIT_BYTES)


def _dot(a, b):
    return jnp.dot(a, b, preferred_element_type=F32)


def _dot_nt(a, b):
    return lax.dot_general(a, b, (((1,), (1,)), ((), ())), preferred_element_type=F32)


def _dot_tn(a, b):
    return lax.dot_general(a, b, (((0,), (0,)), ((), ())), preferred_element_type=F32)


def _lambda_init(layer):
    return 0.8 - 0.6 * math.exp(-0.3 * layer)


def _lam_from(lamv, lam_init):
    t1 = jnp.sum(lamv[0:1, :] * lamv[1:2, :], axis=-1, keepdims=True)
    t2 = jnp.sum(lamv[2:3, :] * lamv[3:4, :], axis=-1, keepdims=True)
    return jnp.exp(t1) - jnp.exp(t2) + lam_init


def _rms(x):
    return x * lax.rsqrt(jnp.mean(x * x, axis=-1, keepdims=True) + RMS_EPS)


def _in_proj_kernel(h_ref, g_ref, w_ref, wglr_ref, wup_ref, bg_ref,
                    qg_ref, kg_ref, vg_ref, la_ref, zg_ref, qd_ref, kd_ref, vd_ref, zd_ref,
                    *, widths, q_scale, kv_heads):
    u = (_rms(h_ref[...]) * g_ref[...]).astype(BF16)
    tm = h_ref.shape[0]
    outs = (qg_ref, kg_ref, vg_ref, zg_ref, qd_ref, kd_ref, vd_ref, zd_ref)
    lo = 0
    for o_ref, wd in zip(outs, widths):
        y = _dot(u, w_ref[:, lo:lo + wd])
        if o_ref is qg_ref:
            y = y * q_scale
        if kv_heads and (o_ref is kd_ref or o_ref is vd_ref):
            for h in range(kv_heads):
                o_ref[pl.ds(h, tm, stride=kv_heads), :] = y[:, h * LANES:(h + 1) * LANES]
        else:
            o_ref[...] = y
        lo += wd
    glr = _dot(u, wglr_ref[...]).astype(BF16)
    gate = _dot(glr, wup_ref[...]) + bg_ref[...]
    log_sig = jnp.minimum(gate, 0.0) - jnp.log1p(jnp.exp(-jnp.abs(gate)))
    la_ref[...] = log_sig / GLA_GATE_NORM


def _in_proj(h, g, w_main, w_glr, w_up, b_gate, *, widths, q_scale, tm, kv_heads=0):
    n, d = h.shape
    gqk, _, gw, _, dqk, _, dw, _ = widths
    out_w = (gqk, gqk, gw, gqk, gw, dqk, dqk, dw, dw)
    row = lambda i: (i, 0)
    fixed = lambda i: (0, 0)
    shapes = [(n, w) for w in out_w]
    blocks = [(tm, w) for w in out_w]
    if kv_heads:
        for i in (6, 7):
            shapes[i] = (n * kv_heads, out_w[i] // kv_heads)
            blocks[i] = (tm * kv_heads, out_w[i] // kv_heads)
    return pl.pallas_call(
        functools.partial(_in_proj_kernel, widths=widths, q_scale=q_scale, kv_heads=kv_heads),
        grid=(n // tm,),
        in_specs=[pl.BlockSpec((tm, d), row), pl.BlockSpec((1, d), fixed),
                  pl.BlockSpec(w_main.shape, fixed), pl.BlockSpec(w_glr.shape, fixed),
                  pl.BlockSpec(w_up.shape, fixed), pl.BlockSpec(b_gate.shape, fixed)],
        out_specs=[pl.BlockSpec(b, row) for b in blocks],
        out_shape=[jax.ShapeDtypeStruct(s, F32) for s in shapes],
        compiler_params=_cparams("parallel"),
        name="in_proj",
    )(h, g, w_main, w_glr, w_up, b_gate)


def _gla_prompt_kernel(q_ref, k_ref, v_ref, la_ref, o_ref, s_ref, S_scr, *, n_meta, n_chunks, dk, dv):
    C = GLA_CHUNK
    S_scr[...] = jnp.zeros_like(S_scr)
    tril = lax.broadcasted_iota(jnp.int32, (C, C), 0) >= lax.broadcasted_iota(jnp.int32, (C, C), 1)
    tri = tril.astype(F32)
    head_a = lax.broadcasted_iota(jnp.int32, (1, 2 * dk), 1) < dk
    block_diag = ((lax.broadcasted_iota(jnp.int32, (2 * dk, 2 * dv), 0) < dk)
                  == (lax.broadcasted_iota(jnp.int32, (2 * dk, 2 * dv), 1) < dv))
    row_col = lax.broadcasted_iota(jnp.int32, (C, 1), 0)

    def chunk(r0, n_valid):
        la = la_ref[pl.ds(r0, C), :]
        q = q_ref[pl.ds(r0, C), :]
        k = k_ref[pl.ds(r0, C), :]
        v = v_ref[pl.ds(r0, C), :]
        if n_valid < C:
            keep = row_col < n_valid
            la = jnp.where(keep, la, 0.0)
            k = jnp.where(keep, k, 0.0)
            v = jnp.where(keep, v, 0.0)
        b = jnp.dot(tri, la, preferred_element_type=F32, precision=lax.Precision.HIGHEST)
        b_last = b[C - 1:C, :]
        q_in = (q * jnp.exp(b)).astype(BF16)
        k_in = (k * jnp.exp(-b)).astype(BF16)
        k_st = (k * jnp.exp(b_last - b)).astype(BF16)
        vb = v.astype(BF16)
        zero = jnp.zeros_like(q_in)
        att_a = jnp.where(tril, _dot_nt(jnp.where(head_a, q_in, zero), k_in), 0.0).astype(BF16)
        att_b = jnp.where(tril, _dot_nt(jnp.where(head_a, zero, q_in), k_in), 0.0).astype(BF16)
        o_intra = jnp.concatenate([_dot(att_a, vb[:, :dv]), _dot(att_b, vb[:, dv:])], axis=-1)
        S = S_scr[...]
        o = o_intra + _dot(q_in, S.astype(BF16))
        la_sq = jnp.concatenate([la, jnp.zeros((2 * dk - C, 2 * dk), F32)], axis=0)
        decay = jnp.exp(jnp.sum(la_sq.T, axis=-1, keepdims=True))
        ds = jnp.where(block_diag, _dot_tn(k_st, vb), 0.0)
        S_scr[...] = S * decay + ds
        return o

    o0 = chunk(0, n_meta)
    o_ref[0:n_meta, :] = o0[0:n_meta, :]

    def body(c, carry):
        r0 = pl.multiple_of(n_meta + c * C, 8)
        o_ref[pl.ds(r0, C), :] = chunk(r0, C)
        return carry

    lax.fori_loop(0, n_chunks, body, 0, unroll=4)
    S = S_scr[...]
    s_ref[0] = S[0:dk, 0:dv]
    s_ref[1] = S[dk:2 * dk, dv:2 * dv]


def _gla_prompt(qg, kg, vg, la, *, batch, seq_len, n_meta, heads, dk, dv):
    n_chunks = (seq_len - n_meta) // GLA_CHUNK
    qk_spec = pl.BlockSpec((seq_len, 2 * dk), lambda b, p: (b, p))
    v_spec = pl.BlockSpec((seq_len, 2 * dv), lambda b, p: (b, p))
    return pl.pallas_call(
        functools.partial(_gla_prompt_kernel, n_meta=n_meta, n_chunks=n_chunks, dk=dk, dv=dv),
        grid=(batch, heads // 2),
        in_specs=[qk_spec, qk_spec, v_spec, qk_spec],
        out_specs=[v_spec, pl.BlockSpec((None, 2, dk, dv), lambda b, p: (b, p, 0, 0))],
        out_shape=[jax.ShapeDtypeStruct((batch * seq_len, heads * dv), F32),
                   jax.ShapeDtypeStruct((batch, heads, dk, dv), F32)],
        scratch_shapes=[pltpu.VMEM((2 * dk, 2 * dv), F32)],
        compiler_params=_cparams("parallel", "parallel"),
        name="gla_prompt",
    )(qg, kg, vg, la)


def _gla_decode_kernel(qT_ref, kT_ref, laT_ref, v_ref, s0_ref, o_ref, s_ref, *, heads, dk, dv, steps):
    for h in range(heads):
        S = s0_ref[h]
        rows = slice(h * dk, (h + 1) * dk)
        cols = slice(h * dv, (h + 1) * dv)
        for t in range(steps):
            S = S * jnp.exp(laT_ref[rows, t:t + 1]) + kT_ref[rows, t:t + 1] * v_ref[t:t + 1, cols]
            o_ref[t:t + 1, cols] = jnp.sum(qT_ref[rows, t:t + 1] * S, axis=0, keepdims=True)
        s_ref[h] = S


def _gla_decode(qT, kT, laT, v, state, layer, *, heads, dk, dv):
    db, _, steps = qT.shape
    t_spec = pl.BlockSpec((None, heads * dk, steps), lambda b: (b, 0, 0))
    v_spec = pl.BlockSpec((None, steps, heads * dv), lambda b: (b, 0, 0))
    return pl.pallas_call(
        functools.partial(_gla_decode_kernel, heads=heads, dk=dk, dv=dv, steps=steps),
        grid=(db,),
        in_specs=[t_spec, t_spec, t_spec, v_spec,
                  pl.BlockSpec((None, None, heads, dk, dv), lambda b: (layer, b, 0, 0, 0))],
        out_specs=[v_spec, pl.BlockSpec((None, heads, dk, dv), lambda b: (b, 0, 0, 0))],
        out_shape=[jax.ShapeDtypeStruct((db, steps, heads * dv), F32),
                   jax.ShapeDtypeStruct((db, heads, dk, dv), F32)],
        compiler_params=_cparams("parallel"),
        name="gla_decode",
    )(qT, kT, laT, v, state)


def _attn_prompt_kernel(slopes_ref, lamv_ref, q_ref, k_ref, v_ref, o_ref,
                        kb_scr, vb_scr, qm_scr, m_scr, acc_scr, *, heads, n_q, tq, dh, lam_init):
    seq_len = q_ref.shape[0]
    pad_len = kb_scr.shape[1]
    dv = v_ref.shape[1]
    tail = seq_len - n_q * tq
    lam = _lam_from(lamv_ref[...], lam_init)
    first = lax.broadcasted_iota(jnp.int32, (1, LANES), 1) < dh
    for h in range(heads):
        kb_scr[h, 0:seq_len, :] = k_ref[pl.ds(h, seq_len, stride=heads), :].astype(BF16)
        vb_scr[h, 0:seq_len, 0:dv] = v_ref[pl.ds(h, seq_len, stride=heads), :].astype(BF16)
        vb_scr[h, 0:seq_len, dv:2 * dv] = jnp.ones((seq_len, dv), BF16)
        kb_scr[h, seq_len:pad_len, :] = jnp.zeros((pad_len - seq_len, LANES), BF16)
        vb_scr[h, seq_len:pad_len, :] = jnp.zeros((pad_len - seq_len, 2 * dv), BF16)
    scale = dh ** -0.5 * LOG2E

    def attend(q0, rows, n_full, diag_width):
        r2 = 2 * rows
        for h in range(heads):
            q = (q_ref[pl.ds(q0, rows), h * LANES:(h + 1) * LANES] * scale).astype(BF16)
            zero = jnp.zeros_like(q)
            qm_scr[h, 0:rows, :] = jnp.where(first, q, zero)
            qm_scr[h, rows:r2, :] = jnp.where(first, zero, q)
            m_scr[h, 0:r2, :] = jnp.full((r2, LANES), NEG_INF, F32)
            acc_scr[h, 0:r2, :] = jnp.zeros((r2, 2 * dv), F32)
        q_pos = q0 + lax.rem(lax.broadcasted_iota(jnp.int32, (r2, 1), 0), rows)

        def step(r0, width, causal):
            rel = (r0 + lax.broadcasted_iota(jnp.int32, (1, width), 1) - q0)
            for h in range(heads):
                s = (_dot_nt(qm_scr[h, 0:r2, :], kb_scr[h, pl.ds(r0, width), :])
                     + (slopes_ref[h] * LOG2E) * rel.astype(F32))
                if causal:
                    s = jnp.where(rel + q0 <= q_pos, s, NEG_INF)
                m_prev = m_scr[h, 0:r2, :]
                m_new = jnp.maximum(m_prev, jnp.max(s, axis=-1, keepdims=True))
                alpha = jnp.exp2(m_prev - m_new)
                p = jnp.exp2(s - jnp.concatenate([m_new] * (width // LANES), axis=-1)).astype(BF16)
                acc_scr[h, 0:r2, :] = (acc_scr[h, 0:r2, :] * jnp.concatenate([alpha, alpha], axis=-1)
                                       + _dot(p, vb_scr[h, pl.ds(r0, width), :]))
                m_scr[h, 0:r2, :] = m_new

        def pair(j, carry):
            step(pl.multiple_of(j * (2 * tq), 2 * tq), 2 * tq, False)
            return carry

        lax.fori_loop(0, n_full // 2, pair, 0)
        if isinstance(n_full, int):
            if n_full % 2:
                step((n_full - 1) * tq, tq, False)
        else:
            @pl.when(n_full % 2 == 1)
            def _():
                step(pl.multiple_of((n_full - 1) * tq, tq), tq, False)
        step(q0, diag_width, True)
        for h in range(heads):
            acc = acc_scr[h, 0:r2, :]
            o = acc[:, 0:dv] / acc[:, dv:2 * dv]
            o_ref[pl.ds(q0, rows), h * dv:(h + 1) * dv] = o[0:rows, :] - lam * o[rows:r2, :]

    def q_body(i, carry):
        attend(pl.multiple_of(i * tq, tq), tq, i, tq)
        return carry

    lax.fori_loop(0, n_q, q_body, 0)
    if tail:
        attend(n_q * tq, tail, n_q, LANES)


def _attn_prompt(slopes, lamv, qd, k_rows, v_rows, *, batch, seq_len, heads, dh, dv, lam_init, tq):
    n_q = seq_len // tq
    pad_len = n_q * tq + LANES if seq_len % tq else seq_len
    assert seq_len - n_q * tq <= LANES
    tok_spec = pl.BlockSpec((seq_len, heads * dv), lambda b: (b, 0))
    row_spec = pl.BlockSpec((seq_len * heads, dv), lambda b: (b, 0))
    return pl.pallas_call(
        functools.partial(_attn_prompt_kernel, heads=heads, n_q=n_q, tq=tq, dh=dh, lam_init=lam_init),
        grid=(batch,),
        in_specs=[pl.BlockSpec(memory_space=pltpu.SMEM), pl.BlockSpec(lamv.shape, lambda b: (0, 0)),
                  tok_spec, row_spec, row_spec],
        out_specs=tok_spec,
        out_shape=jax.ShapeDtypeStruct((batch * seq_len, heads * dv), F32),
        scratch_shapes=[pltpu.VMEM((heads, pad_len, 2 * dh), BF16), pltpu.VMEM((heads, pad_len, 2 * dv), BF16),
                        pltpu.VMEM((heads, 2 * tq, 2 * dh), BF16),
                        pltpu.VMEM((heads, 2 * tq, LANES), F32), pltpu.VMEM((heads, 2 * tq, 2 * dv), F32)],
        compiler_params=_cparams("parallel"),
        name="attn_prompt",
    )(slopes, lamv, qd, k_rows, v_rows)


def _attn_decode_kernel(pt_ref, qm_ref, base_ref, slope_ref, newbase_ref, kn_ref, vn_ref, lamv_ref, *rest,
                        pages_per_step, page, past, lam_init):
    del pt_ref
    k_refs = rest[:pages_per_step]
    v_refs = rest[pages_per_step:2 * pages_per_step]
    o_ref, m_scr, l_scr, acc_scr = rest[2 * pages_per_step:]
    j = pl.program_id(1)
    half = o_ref.shape[0]

    @pl.when(j == 0)
    def _():
        m_scr[...] = jnp.full_like(m_scr, NEG_INF)
        l_scr[...] = jnp.zeros_like(l_scr)
        acc_scr[...] = jnp.zeros_like(acc_scr)

    qm = qm_ref[...]
    slope = slope_ref[...]

    def update(scores, value_refs):
        m_prev = m_scr[...]
        tile_max = functools.reduce(jnp.maximum, scores)
        m_new = jnp.maximum(m_prev, jnp.max(tile_max, axis=-1, keepdims=True))
        alpha = jnp.exp2(m_prev - m_new)
        rep = scores[0].shape[1] // LANES
        m_rep = jnp.concatenate([m_new] * rep, axis=-1)
        p_sum = None
        pv = None
        for s, v_ref in zip(scores, value_refs):
            p = jnp.exp2(s - m_rep)
            p_sum = p if p_sum is None else p_sum + p
            d = _dot(p.astype(BF16), v_ref[...].astype(BF16))
            pv = d if pv is None else pv + d
        l_scr[...] = alpha * l_scr[...] + jnp.sum(p_sum, axis=-1, keepdims=True)
        acc_scr[...] = alpha * acc_scr[...] + pv
        m_scr[...] = m_new

    scores = []
    for i in range(pages_per_step):
        pos0 = ((j * pages_per_step + i) * page - past).astype(F32)
        scores.append(_dot_nt(qm, k_refs[i][...].astype(BF16)) + (base_ref[...] + slope * pos0))
    update(scores, v_refs)

    @pl.when(j == pl.num_programs(1) - 1)
    def _():
        update([_dot_nt(qm, kn_ref[...].astype(BF16)) + newbase_ref[...]], [vn_ref])
        lam = _lam_from(lamv_ref[...], lam_init)
        o = acc_scr[...] / l_scr[...]
        o_ref[...] = o[0:half, :] - lam * o[half:2 * half, :]


def _attn_decode(page_table, qm, base, slope_rows, newbase, k_new, v_new, lamv, cache_k, cache_v, layer,
                 *, pages_per_step, page, lam_init):
    db, n_pages = page_table.shape
    rows = qm.shape[1]
    page_rows, width = cache_k.shape[2:]
    fixed = lambda a: pl.BlockSpec(a.shape, lambda b, j, pt: (0,) * a.ndim)
    per_seq = lambda a: pl.BlockSpec((None,) + a.shape[1:], lambda b, j, pt: (b, 0, 0))

    def page_spec(i):
        return pl.BlockSpec((None, None, page_rows, width),
                            lambda b, j, pt: (layer, pt[b, j * pages_per_step + i], 0, 0))

    pages = [page_spec(i) for i in range(pages_per_step)]
    grid_spec = pltpu.PrefetchScalarGridSpec(
        num_scalar_prefetch=1,
        grid=(db, n_pages // pages_per_step),
        in_specs=[per_seq(qm), fixed(base), fixed(slope_rows), fixed(newbase),
                  per_seq(k_new), per_seq(v_new), fixed(lamv)] + pages + pages,
        out_specs=pl.BlockSpec((None, rows // 2, width), lambda b, j, pt: (b, 0, 0)),
        scratch_shapes=[pltpu.VMEM((rows, LANES), F32), pltpu.VMEM((rows, LANES), F32),
                        pltpu.VMEM((rows, width), F32)],
    )
    return pl.pallas_call(
        functools.partial(_attn_decode_kernel, pages_per_step=pages_per_step, page=page,
                          past=n_pages * page, lam_init=lam_init),
        grid_spec=grid_spec,
        out_shape=jax.ShapeDtypeStruct((db, rows // 2, width), F32),
        compiler_params=_cparams("parallel", "arbitrary"),
        name="attn_decode",
    )(page_table, qm, base, slope_rows, newbase, k_new, v_new, lamv,
      *([cache_k] * pages_per_step), *([cache_v] * pages_per_step))


def _mix_out_kernel(h_ref, og_ref, zg_ref, od_ref, zd_ref, gg_ref, gd_ref, w_ref, o_ref,
                    *, heads_g, heads_d, dv_g, dv_d, diff_scale):
    parts = []
    for h in range(heads_g):
        c = slice(h * dv_g, (h + 1) * dv_g)
        z = zg_ref[:, c]
        parts.append((_rms(og_ref[:, c]) * gg_ref[...]) * (z * jax.nn.sigmoid(z)))
    for h in range(heads_d):
        c = slice(h * dv_d, (h + 1) * dv_d)
        z = zd_ref[:, c]
        parts.append(((_rms(od_ref[:, c]) * gd_ref[...]) * diff_scale) * (z * jax.nn.sigmoid(z)))
    mix = jnp.concatenate(parts, axis=-1).astype(BF16)
    o_ref[...] = h_ref[...] + _dot(mix, w_ref[...])


def _mix_out(h, o_gla, zg, o_diff, zd, g_gla, g_diff, w_out, *, dv_g, dv_d, diff_scale, tm):
    n, d = h.shape
    row = lambda i: (i, 0)
    fixed = lambda i: (0, 0)
    blk = lambda a: pl.BlockSpec((tm, a.shape[1]), row)
    return pl.pallas_call(
        functools.partial(_mix_out_kernel, heads_g=o_gla.shape[1] // dv_g, heads_d=o_diff.shape[1] // dv_d,
                          dv_g=dv_g, dv_d=dv_d, diff_scale=diff_scale),
        grid=(n // tm,),
        in_specs=[blk(h), blk(o_gla), blk(zg), blk(o_diff), blk(zd),
                  pl.BlockSpec(g_gla.shape, fixed), pl.BlockSpec(g_diff.shape, fixed),
                  pl.BlockSpec(w_out.shape, fixed)],
        out_specs=blk(h),
        out_shape=jax.ShapeDtypeStruct((n, d), F32),
        compiler_params=_cparams("parallel"),
        name="mix_out",
    )(h, o_gla, zg, o_diff, zd, g_gla, g_diff, w_out)


def _final_norm_kernel(h_ref, g_ref, o_ref, *, skip, chunk):
    def body(i, carry):
        x = h_ref[pl.ds(pl.multiple_of(skip + i * chunk, 8), chunk), :]
        o_ref[pl.ds(pl.multiple_of(i * chunk, 8), chunk), :] = _rms(x) * g_ref[...]
        return carry

    lax.fori_loop(0, o_ref.shape[0] // chunk, body, 0)


def _final_norm(h3, g, *, skip, chunk):
    b, l, d = h3.shape
    return pl.pallas_call(
        functools.partial(_final_norm_kernel, skip=skip, chunk=chunk),
        grid=(b,),
        in_specs=[pl.BlockSpec((None, l, d), lambda i: (i, 0, 0)), pl.BlockSpec(g.shape, lambda i: (0, 0))],
        out_specs=pl.BlockSpec((None, l - skip, d), lambda i: (i, 0, 0)),
        out_shape=jax.ShapeDtypeStruct((b, l - skip, d), F32),
        compiler_params=_cparams("parallel"),
        name="final_norm",
    )(h3, g)


def _row_tile(n, cap):
    best = 8
    for t in range(8, cap + 1, 8):
        if n % t == 0:
            best = t
    return best


def kernel(x_prompt, x_sample, cache_k, cache_v, state_gla, page_table, meta_tokens, norm_g, w_in, gla_w_up, gla_b_gate, gla_norm_g, lambda_q1, lambda_k1, lambda_q2, lambda_k2, diff_norm_g, w_out, final_norm_g):
    B, SEQ, D = x_prompt.shape
    DB, T, _ = x_sample.shape
    DEPTH, NPOOL, PAGE, HD, DK2 = cache_k.shape
    DV = cache_v.shape[-1]
    DH = DK2 // 2
    NM = meta_tokens.shape[0]
    HG, GDK, GDV = state_gla.shape[2:]
    RANK = gla_w_up.shape[1]
    L = NM + SEQ
    NP, NS = B * L, DB * T
    GQK, GW, DQK, DW = HG * GDK, HG * GDV, HD * DK2, HD * DV
    widths = (GQK, GQK, GW, GW, DQK, DQK, DW, DW)
    assert w_in.shape[-1] == sum(widths) + RANK and DK2 == LANES and DV == LANES and 2 * GDK == LANES

    off = 2 * GQK + GW
    w_main = jnp.concatenate([w_in[..., :off], w_in[..., off + RANK:]], axis=-1).astype(BF16)
    w_glr = jnp.pad(w_in[..., off:off + RANK], ((0, 0), (0, 0), (0, LANES - RANK))).astype(BF16)
    w_up = jnp.pad(gla_w_up, ((0, 0), (0, LANES - RANK), (0, 0))).astype(BF16)
    w_out_b = w_out.astype(BF16)
    lamv = jnp.stack([lambda_q1, lambda_k1, lambda_q2, lambda_k2], axis=1)
    slopes = 2.0 ** (-8.0 * jnp.arange(1, HD + 1, dtype=F32) / HD)
    cache_k2 = cache_k.reshape(DEPTH, NPOOL, PAGE * HD, DK2)
    cache_v2 = cache_v.reshape(DEPTH, NPOOL, PAGE * HD, DV)

    rows = 2 * HD * T
    row_h = (jnp.arange(rows) // T) % HD
    row_t = jnp.arange(rows) % T
    row_slope = (slopes[row_h] * LOG2E)[:, None]
    slope_rows = jnp.broadcast_to(row_slope, (rows, PAGE * HD))
    col = jnp.arange(PAGE * HD)
    base = jnp.where((col % HD)[None, :] == row_h[:, None], row_slope * (col // HD)[None, :].astype(F32), NEG_INF)
    ncol = jnp.arange(LANES)
    new_ok = (ncol[None, :] < T * HD) & ((ncol % HD)[None, :] == row_h[:, None]) & ((ncol // HD)[None, :] <= row_t[:, None])
    newbase = jnp.where(new_ok, row_slope * (ncol // HD)[None, :].astype(F32), NEG_INF)

    hp = jnp.concatenate([jnp.broadcast_to(meta_tokens[None], (B, NM, D)), x_prompt], axis=1).reshape(NP, D)
    hs = x_sample.reshape(NS, D)
    tm_p = _row_tile(NP, 512)
    n_pages = page_table.shape[1]
    pages_per_step = max(p for p in range(1, 17) if n_pages % p == 0)
    pk, pv, ps, sk, sv, ss = [], [], [], [], [], []
    for l in range(DEPTH):
        lam_init = _lambda_init(l)
        proj_args = (norm_g[l][None], w_main[l], w_glr[l], w_up[l], gla_b_gate[l][None])
        qg, kg, vg, la, zg, qd, kd, vd, zd = _in_proj(hp, *proj_args, widths=widths, q_scale=GDK ** -0.5, tm=tm_p,
                                                      kv_heads=HD)
        o_gla, s_p = _gla_prompt(qg, kg, vg, la, batch=B, seq_len=L, n_meta=NM, heads=HG, dk=GDK, dv=GDV)
        o_diff = _attn_prompt(slopes, lamv[l], qd, kd, vd, batch=B, seq_len=L, heads=HD,
                              dh=DH, dv=DV, lam_init=lam_init, tq=2 * LANES)
        hp = _mix_out(hp, o_gla, zg, o_diff, zd, gla_norm_g[l][None], diff_norm_g[l][None], w_out_b[l],
                      dv_g=GDV, dv_d=DV, diff_scale=1.0 - lam_init, tm=tm_p)
        pk.append(kd.reshape(B, L, HD, DK2))
        pv.append(vd.reshape(B, L, HD, DV))
        ps.append(s_p)
        qg, kg, vg, la, zg, qd, kd, vd, zd = _in_proj(hs, *proj_args, widths=widths, q_scale=GDK ** -0.5, tm=NS)
        tr = lambda a: a.reshape(DB, T, -1).transpose(0, 2, 1)
        o_gla, s_s = _gla_decode(tr(qg), tr(kg), tr(la), vg.reshape(DB, T, GW), state_gla, l,
                                 heads=HG, dk=GDK, dv=GDV)
        q5 = qd.reshape(DB, T, HD, 2, DH) * (DH ** -0.5 * LOG2E)
        zeros = jnp.zeros_like(q5[..., 0, :])
        qm = jnp.stack([jnp.concatenate([q5[..., 0, :], zeros], -1), jnp.concatenate([zeros, q5[..., 1, :]], -1)], 1)
        qm = qm.transpose(0, 1, 3, 2, 4).reshape(DB, rows, DK2).astype(BF16)
        pad_new = lambda a: jnp.pad(a.reshape(DB, T * HD, LANES), ((0, 0), (0, LANES - T * HD), (0, 0)))
        o_dec = _attn_decode(page_table, qm, base, slope_rows, newbase, pad_new(kd), pad_new(vd), lamv[l],
                             cache_k2, cache_v2, l, pages_per_step=pages_per_step, page=PAGE, lam_init=lam_init)
        o_diff = o_dec.reshape(DB, HD, T, DV).transpose(0, 2, 1, 3).reshape(NS, DW)
        hs = _mix_out(hs, o_gla.reshape(NS, GW), zg, o_diff, zd, gla_norm_g[l][None], diff_norm_g[l][None],
                      w_out_b[l], dv_g=GDV, dv_d=DV, diff_scale=1.0 - lam_init, tm=NS)
        sk.append(kd.reshape(DB, T, HD, DK2))
        sv.append(vd.reshape(DB, T, HD, DV))
        ss.append(s_s)
    y_prompt = _final_norm(hp.reshape(B, L, D), final_norm_g[None], skip=NM, chunk=LANES)
    y_sample = _final_norm(hs.reshape(1, NS, D), final_norm_g[None], skip=0, chunk=NS).reshape(DB, T, D)
    return (y_prompt, y_sample, jnp.stack(pk), jnp.stack(pv), jnp.stack(ps),
            jnp.stack(sk), jnp.stack(sv), jnp.stack(ss))
```

```python
import functools
import math

import jax
import jax.numpy as jnp
from jax import lax
from jax.experimental import pallas as pl
from jax.experimental.pallas import tpu as pltpu

RMS_EPS = 1e-6
NEG_INF = -1e30
GLA_GATE_NORM = 16.0
GLA_CHUNK = 64
LOG2E = 1.4426950408889634
LANES = 128
VMEM_LIMIT_BYTES = 56 * 1024 * 1024
BF16 = jnp.bfloat16
F32 = jnp.float32


def _cparams(*sem):
    return pltpu.CompilerParams(dimension_semantics=sem, vmem_limit_bytes=VMEM_LIMIT_BYTES)


def _dot(a, b):
    return jnp.dot(a, b, preferred_element_type=F32)


def _dot_nt(a, b):
    return lax.dot_general(a, b, (((1,), (1,)), ((), ())), preferred_element_type=F32)


def _dot_tn(a, b):
    return lax.dot_general(a, b, (((0,), (0,)), ((), ())), preferred_element_type=F32)


def _lambda_init(layer):
    return 0.8 - 0.6 * math.exp(-0.3 * layer)


def _lam_from(lamv, lam_init):
    t1 = jnp.sum(lamv[0:1, :] * lamv[1:2, :], axis=-1, keepdims=True)
    t2 = jnp.sum(lamv[2:3, :] * lamv[3:4, :], axis=-1, keepdims=True)
    return jnp.exp(t1) - jnp.exp(t2) + lam_init


def _rms(x):
    return x * lax.rsqrt(jnp.mean(x * x, axis=-1, keepdims=True) + RMS_EPS)


def _row_stage_kernel(*refs, mix, proj, widths, q_scale, kv_heads, dv_g, dv_d, diff_scale):
    refs = list(refs)
    x = refs.pop(0)[...]
    if mix:
        og_ref, zg_ref, od_ref, zd_ref, gg_ref, gd_ref, wo_ref = refs[:7]
        del refs[:7]
    if proj:
        g_ref, w_ref, wglr_ref, wup_ref, bg_ref = refs[:5]
        del refs[:5]
    if mix:
        parts = []
        for o_ref, z_ref, gain_ref, dv, post in ((og_ref, zg_ref, gg_ref, dv_g, 1.0),
                                                  (od_ref, zd_ref, gd_ref, dv_d, diff_scale)):
            for h in range(o_ref.shape[1] // dv):
                c = slice(h * dv, (h + 1) * dv)
                z = z_ref[:, c]
                y = _rms(o_ref[:, c]) * gain_ref[...]
                if post != 1.0:
                    y = y * post
                parts.append(y * (z * jax.nn.sigmoid(z)))
        x = x + _dot(jnp.concatenate(parts, axis=-1).astype(BF16), wo_ref[...])
        refs.pop(0)[...] = x
    if proj:
        qg_ref, kg_ref, vg_ref, la_ref, zg_out, qd_ref, kd_ref, vd_ref, zd_out = refs
        u = (_rms(x) * g_ref[...]).astype(BF16)
        tm = x.shape[0]
        lo = 0
        for o_ref, wd in zip((qg_ref, kg_ref, vg_ref, zg_out, qd_ref, kd_ref, vd_ref, zd_out), widths):
            y = _dot(u, w_ref[:, lo:lo + wd])
            if o_ref is qg_ref:
                y = y * q_scale
            if kv_heads and (o_ref is kd_ref or o_ref is vd_ref):
                for h in range(kv_heads):
                    o_ref[pl.ds(h, tm, stride=kv_heads), :] = y[:, h * LANES:(h + 1) * LANES]
            else:
                o_ref[...] = y
            lo += wd
        glr = _dot(u, wglr_ref[...]).astype(BF16)
        gate = _dot(glr, wup_ref[...]) + bg_ref[...]
        log_sig = jnp.minimum(gate, 0.0) - jnp.log1p(jnp.exp(-jnp.abs(gate)))
        la_ref[...] = log_sig / GLA_GATE_NORM


def _row_stage(h, mix_args, proj_args, *, widths, q_scale, tm, kv_heads, dv_g, dv_d, diff_scale):
    n, d = h.shape
    row = lambda i: (i, 0)
    fixed = lambda i: (0, 0)
    rows = lambda a: pl.BlockSpec((tm, a.shape[1]), row)
    whole = lambda a: pl.BlockSpec(a.shape, fixed)
    args, in_specs, out_specs, out_shapes = [h], [rows(h)], [], []
    if mix_args is not None:
        args += list(mix_args)
        in_specs += [rows(a) for a in mix_args[:4]] + [whole(a) for a in mix_args[4:]]
        out_specs.append(pl.BlockSpec((tm, d), row))
        out_shapes.append(jax.ShapeDtypeStruct((n, d), F32))
    if proj_args is not None:
        args += list(proj_args)
        in_specs += [whole(a) for a in proj_args]
        gqk, _, gw, _, dqk, _, dw, _ = widths
        for i, w in enumerate((gqk, gqk, gw, gqk, gw, dqk, dqk, dw, dw)):
            split = kv_heads if (kv_heads and i in (6, 7)) else 1
            out_specs.append(pl.BlockSpec((tm * split, w // split), row))
            out_shapes.append(jax.ShapeDtypeStruct((n * split, w // split), F32))
    return pl.pallas_call(
        functools.partial(_row_stage_kernel, mix=mix_args is not None, proj=proj_args is not None,
                          widths=widths, q_scale=q_scale, kv_heads=kv_heads, dv_g=dv_g, dv_d=dv_d,
                          diff_scale=diff_scale),
        grid=(n // tm,),
        in_specs=in_specs,
        out_specs=out_specs,
        out_shape=out_shapes,
        compiler_params=_cparams("parallel"),
        name="row_stage",
    )(*args)


def _gla_prompt_kernel(q_ref, k_ref, v_ref, la_ref, o_ref, s_ref, st_scr, *, n_meta, n_chunks, dk, dv, pairs):
    C = GLA_CHUNK
    st_scr[...] = jnp.zeros_like(st_scr)
    tril2 = (lax.broadcasted_iota(jnp.int32, (C, 2 * C), 0)
             >= lax.rem(lax.broadcasted_iota(jnp.int32, (C, 2 * C), 1), C))
    head_a = lax.broadcasted_iota(jnp.int32, (1, 2 * dk), 1) < dk
    val_a = lax.broadcasted_iota(jnp.int32, (1, 2 * dv), 1) < dv
    block_diag = ((lax.broadcasted_iota(jnp.int32, (2 * dv, 2 * dk), 0) < dv)
                  == (lax.broadcasted_iota(jnp.int32, (2 * dv, 2 * dk), 1) < dk))
    row = lax.broadcasted_iota(jnp.int32, (C, 1), 0)

    def cumsum_rows(x):
        shift = 1
        while shift < C:
            x = x + jnp.where(row >= shift, pltpu.roll(x, shift, axis=0), 0.0)
            shift *= 2
        return x

    def chunk(r0, n_valid):
        outs = []
        for p in range(pairs):
            qk_l = slice(p * 2 * dk, (p + 1) * 2 * dk)
            v_l = slice(p * 2 * dv, (p + 1) * 2 * dv)
            la = la_ref[pl.ds(r0, C), qk_l]
            q = q_ref[pl.ds(r0, C), qk_l]
            k = k_ref[pl.ds(r0, C), qk_l]
            v = v_ref[pl.ds(r0, C), v_l]
            if n_valid < C:
                keep = row < n_valid
                la = jnp.where(keep, la, 0.0)
                k = jnp.where(keep, k, 0.0)
                v = jnp.where(keep, v, 0.0)
            b = cumsum_rows(la)
            b_last = b[C - 1:C, :]
            q_in = (q * jnp.exp(b)).astype(BF16)
            k_in = (k * jnp.exp(-b)).astype(BF16)
            k_st = (k * jnp.exp(b_last - b)).astype(BF16)
            vb = v.astype(BF16)
            zk = jnp.zeros_like(k_in)
            k_bd = jnp.concatenate([jnp.where(head_a, k_in, zk), jnp.where(head_a, zk, k_in)], axis=0)
            att = jnp.where(tril2, _dot_nt(q_in, k_bd), 0.0).astype(BF16)
            zv = jnp.zeros_like(vb)
            v_bd = jnp.concatenate([jnp.where(val_a, vb, zv), jnp.where(val_a, zv, vb)], axis=0)
            o_intra = _dot(att, v_bd)
            st = st_scr[p]
            outs.append(o_intra + _dot_nt(q_in, st.astype(BF16)))
            d_st = jnp.where(block_diag, _dot_tn(vb, k_st), 0.0)
            st_scr[p] = st * jnp.exp(b_last) + d_st
        return jnp.concatenate(outs, axis=-1)

    o0 = chunk(0, n_meta)
    o_ref[0:n_meta, :] = o0[0:n_meta, :]

    def body(c, carry):
        r0 = pl.multiple_of(n_meta + c * C, 8)
        o_ref[pl.ds(r0, C), :] = chunk(r0, C)
        return carry

    lax.fori_loop(0, n_chunks, body, 0, unroll=4)
    for p in range(pairs):
        S = st_scr[p].T
        s_ref[2 * p] = S[0:dk, 0:dv]
        s_ref[2 * p + 1] = S[dk:2 * dk, dv:2 * dv]


def _gla_prompt(qg, kg, vg, la, *, batch, seq_len, n_meta, heads, dk, dv):
    n_chunks = (seq_len - n_meta) // GLA_CHUNK
    qk_spec = pl.BlockSpec((seq_len, heads * dk), lambda b: (b, 0))
    v_spec = pl.BlockSpec((seq_len, heads * dv), lambda b: (b, 0))
    return pl.pallas_call(
        functools.partial(_gla_prompt_kernel, n_meta=n_meta, n_chunks=n_chunks, dk=dk, dv=dv, pairs=heads // 2),
        grid=(batch,),
        in_specs=[qk_spec, qk_spec, v_spec, qk_spec],
        out_specs=[v_spec, pl.BlockSpec((None, heads, dk, dv), lambda b: (b, 0, 0, 0))],
        out_shape=[jax.ShapeDtypeStruct((batch * seq_len, heads * dv), F32),
                   jax.ShapeDtypeStruct((batch, heads, dk, dv), F32)],
        scratch_shapes=[pltpu.VMEM((heads // 2, 2 * dv, 2 * dk), F32)],
        compiler_params=_cparams("parallel"),
        name="gla_prompt",
    )(qg, kg, vg, la)


def _gla_decode_kernel(qklT_ref, v_ref, s0_ref, o_ref, s_ref, *, heads, dk, dv, steps):
    hk = heads * dk
    for b in range(qklT_ref.shape[0]):
        for h in range(heads):
            S = s0_ref[b, h]
            cols = slice(h * dv, (h + 1) * dv)
            col = lambda part, t: qklT_ref[b, part * hk + h * dk:part * hk + (h + 1) * dk, t:t + 1]
            for t in range(steps):
                S = S * jnp.exp(col(2, t)) + col(1, t) * v_ref[b, t:t + 1, cols]
                o_ref[b, t:t + 1, cols] = jnp.sum(col(0, t) * S, axis=0, keepdims=True)
            s_ref[b, h] = S


def _gla_decode(qklT, v, state, layer, *, heads, dk, dv, seqs_per_step):
    db, _, steps = qklT.shape
    sb = seqs_per_step
    return pl.pallas_call(
        functools.partial(_gla_decode_kernel, heads=heads, dk=dk, dv=dv, steps=steps),
        grid=(db // sb,),
        in_specs=[pl.BlockSpec((sb,) + qklT.shape[1:], lambda i: (i, 0, 0)),
                  pl.BlockSpec((sb, steps, heads * dv), lambda i: (i, 0, 0)),
                  pl.BlockSpec((None, sb, heads, dk, dv), lambda i: (layer, i, 0, 0, 0))],
        out_specs=[pl.BlockSpec((sb, steps, heads * dv), lambda i: (i, 0, 0)),
                   pl.BlockSpec((sb, heads, dk, dv), lambda i: (i, 0, 0, 0))],
        out_shape=[jax.ShapeDtypeStruct((db, steps, heads * dv), F32),
                   jax.ShapeDtypeStruct((db, heads, dk, dv), F32)],
        compiler_params=_cparams("parallel"),
        name="gla_decode",
    )(qklT, v, state)


def _attn_prompt_kernel(slopes_ref, lamv_ref, q_ref, k_ref, v_ref, o_ref,
                        kb_scr, vb_scr, qm_scr, m_scr, acc_scr, *, heads, n_q, tq, dh, lam_init):
    seq_len = q_ref.shape[0]
    pad_len = kb_scr.shape[1]
    dv = v_ref.shape[1]
    tail = seq_len - n_q * tq
    lam = _lam_from(lamv_ref[...], lam_init)
    first = lax.broadcasted_iota(jnp.int32, (1, LANES), 1) < dh
    for h in range(heads):
        kb_scr[h, 0:seq_len, :] = k_ref[pl.ds(h, seq_len, stride=heads), :].astype(BF16)
        vb_scr[h, 0:seq_len, 0:dv] = v_ref[pl.ds(h, seq_len, stride=heads), :].astype(BF16)
        vb_scr[h, 0:seq_len, dv:2 * dv] = jnp.ones((seq_len, dv), BF16)
        kb_scr[h, seq_len:pad_len, :] = jnp.zeros((pad_len - seq_len, LANES), BF16)
        vb_scr[h, seq_len:pad_len, :] = jnp.zeros((pad_len - seq_len, 2 * dv), BF16)
    scale = dh ** -0.5 * LOG2E

    def attend(q0, rows, n_full, diag_width):
        r2 = 2 * rows
        for h in range(heads):
            q = (q_ref[pl.ds(q0, rows), h * LANES:(h + 1) * LANES] * scale).astype(BF16)
            zero = jnp.zeros_like(q)
            qm_scr[h, 0:rows, :] = jnp.where(first, q, zero)
            qm_scr[h, rows:r2, :] = jnp.where(first, zero, q)
            m_scr[h, 0:r2, :] = jnp.full((r2, LANES), NEG_INF, F32)
            acc_scr[h, 0:r2, :] = jnp.zeros((r2, 2 * dv), F32)
        q_pos = q0 + lax.rem(lax.broadcasted_iota(jnp.int32, (r2, 1), 0), rows)

        def step(r0, width, causal):
            rel = (r0 + lax.broadcasted_iota(jnp.int32, (1, width), 1) - q0)
            for h in range(heads):
                s = (_dot_nt(qm_scr[h, 0:r2, :], kb_scr[h, pl.ds(r0, width), :])
                     + (slopes_ref[h] * LOG2E) * rel.astype(F32))
                if causal:
                    s = jnp.where(rel + q0 <= q_pos, s, NEG_INF)
                m_prev = m_scr[h, 0:r2, :]
                m_new = jnp.maximum(m_prev, jnp.max(s, axis=-1, keepdims=True))
                alpha = jnp.exp2(m_prev - m_new)
                p = jnp.exp2(s - jnp.concatenate([m_new] * (width // LANES), axis=-1)).astype(BF16)
                acc_scr[h, 0:r2, :] = (acc_scr[h, 0:r2, :] * jnp.concatenate([alpha, alpha], axis=-1)
                                       + _dot(p, vb_scr[h, pl.ds(r0, width), :]))
                m_scr[h, 0:r2, :] = m_new

        def pair(j, carry):
            step(pl.multiple_of(j * (2 * tq), 2 * tq), 2 * tq, False)
            return carry

        lax.fori_loop(0, n_full // 2, pair, 0)
        if isinstance(n_full, int):
            if n_full % 2:
                step((n_full - 1) * tq, tq, False)
        else:
            @pl.when(n_full % 2 == 1)
            def _():
                step(pl.multiple_of((n_full - 1) * tq, tq), tq, False)
        step(q0, diag_width, True)
        for h in range(heads):
            acc = acc_scr[h, 0:r2, :]
            o = acc[:, 0:dv] / acc[:, dv:2 * dv]
            o_ref[pl.ds(q0, rows), h * dv:(h + 1) * dv] = o[0:rows, :] - lam * o[rows:r2, :]

    def q_body(i, carry):
        attend(pl.multiple_of(i * tq, tq), tq, i, tq)
        return carry

    lax.fori_loop(0, n_q, q_body, 0)
    if tail:
        attend(n_q * tq, tail, n_q, LANES)


def _attn_prompt(slopes, lamv, qd, k_rows, v_rows, *, batch, seq_len, heads, dh, dv, lam_init, tq):
    n_q = seq_len // tq
    pad_len = n_q * tq + LANES if seq_len % tq else seq_len
    assert seq_len - n_q * tq <= LANES
    tok_spec = pl.BlockSpec((seq_len, heads * dv), lambda b: (b, 0))
    row_spec = pl.BlockSpec((seq_len * heads, dv), lambda b: (b, 0))
    return pl.pallas_call(
        functools.partial(_attn_prompt_kernel, heads=heads, n_q=n_q, tq=tq, dh=dh, lam_init=lam_init),
        grid=(batch,),
        in_specs=[pl.BlockSpec(memory_space=pltpu.SMEM), pl.BlockSpec(lamv.shape, lambda b: (0, 0)),
                  tok_spec, row_spec, row_spec],
        out_specs=tok_spec,
        out_shape=jax.ShapeDtypeStruct((batch * seq_len, heads * dv), F32),
        scratch_shapes=[pltpu.VMEM((heads, pad_len, 2 * dh), BF16), pltpu.VMEM((heads, pad_len, 2 * dv), BF16),
                        pltpu.VMEM((heads, 2 * tq, 2 * dh), BF16),
                        pltpu.VMEM((heads, 2 * tq, LANES), F32), pltpu.VMEM((heads, 2 * tq, 2 * dv), F32)],
        compiler_params=_cparams("parallel"),
        name="attn_prompt",
    )(slopes, lamv, qd, k_rows, v_rows)


def _attn_decode_kernel(pt_ref, q_ref, base_ref, slope_ref, newbase_ref, kn_ref, vn_ref, lamv_ref, *rest,
                        pages_per_step, page, past, lam_init, heads, dh):
    del pt_ref
    k_refs = rest[:pages_per_step]
    v_refs = rest[pages_per_step:2 * pages_per_step]
    o_ref, qm_scr, kn_scr, vn_scr, m_scr, l_scr, acc_scr = rest[2 * pages_per_step:]
    j = pl.program_id(1)
    steps = q_ref.shape[0]
    half = heads * steps

    @pl.when(j == 0)
    def _():
        m_scr[...] = jnp.full_like(m_scr, NEG_INF)
        l_scr[...] = jnp.zeros_like(l_scr)
        acc_scr[...] = jnp.zeros_like(acc_scr)
        first = lax.broadcasted_iota(jnp.int32, (1, 2 * dh), 1) < dh
        for h in range(heads):
            q = q_ref[:, h * 2 * dh:(h + 1) * 2 * dh] * (dh ** -0.5 * LOG2E)
            qm_scr[h * steps:(h + 1) * steps, :] = jnp.where(first, q, 0.0)
            qm_scr[half + h * steps:half + (h + 1) * steps, :] = jnp.where(first, 0.0, q)

    qm = qm_scr[...].astype(BF16)
    slope = slope_ref[...]

    def update(scores, value_refs):
        m_prev = m_scr[...]
        tile_max = functools.reduce(jnp.maximum, scores)
        m_new = jnp.maximum(m_prev, jnp.max(tile_max, axis=-1, keepdims=True))
        alpha = jnp.exp2(m_prev - m_new)
        rep = scores[0].shape[1] // LANES
        m_rep = jnp.concatenate([m_new] * rep, axis=-1)
        p_sum = None
        pv = None
        for s, v_ref in zip(scores, value_refs):
            p = jnp.exp2(s - m_rep)
            p_sum = p if p_sum is None else p_sum + p
            d = _dot(p.astype(BF16), v_ref[...].astype(BF16))
            pv = d if pv is None else pv + d
        l_scr[...] = alpha * l_scr[...] + jnp.sum(p_sum, axis=-1, keepdims=True)
        acc_scr[...] = alpha * acc_scr[...] + pv
        m_scr[...] = m_new

    scores = []
    for i in range(pages_per_step):
        pos0 = ((j * pages_per_step + i) * page - past).astype(F32)
        scores.append(_dot_nt(qm, k_refs[i][...].astype(BF16)) + (base_ref[...] + slope * pos0))
    update(scores, v_refs)

    @pl.when(j == pl.num_programs(1) - 1)
    def _():
        n_new = kn_ref.shape[0]
        kn_scr[...] = jnp.zeros_like(kn_scr)
        vn_scr[...] = jnp.zeros_like(vn_scr)
        kn_scr[0:n_new, :] = kn_ref[...]
        vn_scr[0:n_new, :] = vn_ref[...]
        update([_dot_nt(qm, kn_scr[...].astype(BF16)) + newbase_ref[...]], [vn_scr])
        lam = _lam_from(lamv_ref[...], lam_init)
        o = acc_scr[...] / l_scr[...]
        o = o[0:half, :] - lam * o[half:2 * half, :]
        dv = vn_ref.shape[1]
        for h in range(heads):
            o_ref[:, h * dv:(h + 1) * dv] = o[h * steps:(h + 1) * steps, :]


def _attn_decode(page_table, q_tok, base, slope_rows, newbase, k_new, v_new, lamv, cache_k, cache_v, layer,
                 *, pages_per_step, page, lam_init, heads, dh):
    db, n_pages = page_table.shape
    steps = q_tok.shape[1]
    rows = 2 * heads * steps
    page_rows, width = cache_k.shape[2:]
    fixed = lambda a: pl.BlockSpec(a.shape, lambda b, j, pt: (0,) * a.ndim)
    per_seq = lambda a: pl.BlockSpec((None,) + a.shape[1:], lambda b, j, pt: (b, 0, 0))

    def page_spec(i):
        return pl.BlockSpec((None, None, page_rows, width),
                            lambda b, j, pt: (layer, pt[b, j * pages_per_step + i], 0, 0))

    pages = [page_spec(i) for i in range(pages_per_step)]
    grid_spec = pltpu.PrefetchScalarGridSpec(
        num_scalar_prefetch=1,
        grid=(db, n_pages // pages_per_step),
        in_specs=[per_seq(q_tok), fixed(base), fixed(slope_rows), fixed(newbase),
                  per_seq(k_new), per_seq(v_new), fixed(lamv)] + pages + pages,
        out_specs=pl.BlockSpec((None, steps, heads * width), lambda b, j, pt: (b, 0, 0)),
        scratch_shapes=[pltpu.VMEM((rows, 2 * dh), F32), pltpu.VMEM((LANES, 2 * dh), F32),
                        pltpu.VMEM((LANES, width), F32),
                        pltpu.VMEM((rows, LANES), F32), pltpu.VMEM((rows, LANES), F32),
                        pltpu.VMEM((rows, width), F32)],
    )
    return pl.pallas_call(
        functools.partial(_attn_decode_kernel, pages_per_step=pages_per_step, page=page,
                          past=n_pages * page, lam_init=lam_init, heads=heads, dh=dh),
        grid_spec=grid_spec,
        out_shape=jax.ShapeDtypeStruct((db, steps, heads * width), F32),
        compiler_params=_cparams("parallel", "arbitrary"),
        name="attn_decode",
    )(page_table, q_tok, base, slope_rows, newbase, k_new, v_new, lamv,
      *([cache_k] * pages_per_step), *([cache_v] * pages_per_step))


def _final_norm_kernel(h_ref, g_ref, o_ref, *, skip, chunk):
    def body(i, carry):
        x = h_ref[pl.ds(pl.multiple_of(skip + i * chunk, 8), chunk), :]
        o_ref[pl.ds(pl.multiple_of(i * chunk, 8), chunk), :] = _rms(x) * g_ref[...]
        return carry

    lax.fori_loop(0, o_ref.shape[0] // chunk, body, 0)


def _final_norm(h3, g, *, skip, chunk):
    b, l, d = h3.shape
    return pl.pallas_call(
        functools.partial(_final_norm_kernel, skip=skip, chunk=chunk),
        grid=(b,),
        in_specs=[pl.BlockSpec((None, l, d), lambda i: (i, 0, 0)), pl.BlockSpec(g.shape, lambda i: (0, 0))],
        out_specs=pl.BlockSpec((None, l - skip, d), lambda i: (i, 0, 0)),
        out_shape=jax.ShapeDtypeStruct((b, l - skip, d), F32),
        compiler_params=_cparams("parallel"),
        name="final_norm",
    )(h3, g)


def _row_tile(n, cap):
    best = 8
    for t in range(8, cap + 1, 8):
        if n % t == 0:
            best = t
    return best


def kernel(x_prompt, x_sample, cache_k, cache_v, state_gla, page_table, meta_tokens, norm_g, w_in, gla_w_up, gla_b_gate, gla_norm_g, lambda_q1, lambda_k1, lambda_q2, lambda_k2, diff_norm_g, w_out, final_norm_g):
    B, SEQ, D = x_prompt.shape
    DB, T, _ = x_sample.shape
    DEPTH, NPOOL, PAGE, HD, DK2 = cache_k.shape
    DV = cache_v.shape[-1]
    DH = DK2 // 2
    NM = meta_tokens.shape[0]
    HG, GDK, GDV = state_gla.shape[2:]
    RANK = gla_w_up.shape[1]
    L = NM + SEQ
    NP, NS = B * L, DB * T
    GQK, GW, DQK, DW = HG * GDK, HG * GDV, HD * DK2, HD * DV
    widths = (GQK, GQK, GW, GW, DQK, DQK, DW, DW)
    assert w_in.shape[-1] == sum(widths) + RANK and DK2 == LANES and DV == LANES and 2 * GDK == LANES

    off = 2 * GQK + GW
    w_main = jnp.concatenate([w_in[..., :off], w_in[..., off + RANK:]], axis=-1).astype(BF16)
    w_glr = jnp.pad(w_in[..., off:off + RANK], ((0, 0), (0, 0), (0, LANES - RANK))).astype(BF16)
    w_up = jnp.pad(gla_w_up, ((0, 0), (0, LANES - RANK), (0, 0))).astype(BF16)
    w_out_b = w_out.astype(BF16)
    lamv = jnp.stack([lambda_q1, lambda_k1, lambda_q2, lambda_k2], axis=1)
    slopes = 2.0 ** (-8.0 * jnp.arange(1, HD + 1, dtype=F32) / HD)
    cache_k2 = cache_k.reshape(DEPTH, NPOOL, PAGE * HD, DK2)
    cache_v2 = cache_v.reshape(DEPTH, NPOOL, PAGE * HD, DV)

    rows = 2 * HD * T
    row_h = (jnp.arange(rows) // T) % HD
    row_t = jnp.arange(rows) % T
    row_slope = (slopes[row_h] * LOG2E)[:, None]
    slope_rows = jnp.broadcast_to(row_slope, (rows, PAGE * HD))
    col = jnp.arange(PAGE * HD)
    base = jnp.where((col % HD)[None, :] == row_h[:, None], row_slope * (col // HD)[None, :].astype(F32), NEG_INF)
    ncol = jnp.arange(LANES)
    new_ok = (ncol[None, :] < T * HD) & ((ncol % HD)[None, :] == row_h[:, None]) & ((ncol // HD)[None, :] <= row_t[:, None])
    newbase = jnp.where(new_ok, row_slope * (ncol // HD)[None, :].astype(F32), NEG_INF)

    hp = jnp.concatenate([jnp.broadcast_to(meta_tokens[None], (B, NM, D)), x_prompt], axis=1).reshape(NP, D)
    hs = x_sample.reshape(NS, D)
    tm_p = _row_tile(NP, 512)
    n_pages = page_table.shape[1]
    pages_per_step = max(p for p in range(1, 17) if n_pages % p == 0)
    pk, pv, ps, sk, sv, ss = [], [], [], [], [], []

    def proj_params(l):
        return (norm_g[l][None], w_main[l], w_glr[l], w_up[l], gla_b_gate[l][None])

    def stage(h, mix_args, l_next, lam_init, tm, kv_heads):
        return _row_stage(h, mix_args, proj_params(l_next) if l_next < DEPTH else None, widths=widths,
                          q_scale=GDK ** -0.5, tm=tm, kv_heads=kv_heads, dv_g=GDV, dv_d=DV,
                          diff_scale=1.0 - lam_init)

    proj_p = stage(hp, None, 0, 0.0, tm_p, HD)
    proj_s = stage(hs, None, 0, 0.0, NS, 0)
    for l in range(DEPTH):
        lam_init = _lambda_init(l)
        gains = (gla_norm_g[l][None], diff_norm_g[l][None], w_out_b[l])
        qg, kg, vg, la, zg, qd, kd, vd, zd = proj_p
        o_gla, s_p = _gla_prompt(qg, kg, vg, la, batch=B, seq_len=L, n_meta=NM, heads=HG, dk=GDK, dv=GDV)
        o_diff = _attn_prompt(slopes, lamv[l], qd, kd, vd, batch=B, seq_len=L, heads=HD,
                              dh=DH, dv=DV, lam_init=lam_init, tq=2 * LANES)
        hp, *proj_p = stage(hp, (o_gla, zg, o_diff, zd) + gains, l + 1, lam_init, tm_p, HD)
        pk.append(kd.reshape(B, L, HD, DK2))
        pv.append(vd.reshape(B, L, HD, DV))
        ps.append(s_p)
        qg, kg, vg, la, zg, qd, kd, vd, zd = proj_s
        qklT = jnp.concatenate([qg, kg, la], axis=-1).reshape(DB, T, 3 * GQK).transpose(0, 2, 1)
        o_gla, s_s = _gla_decode(qklT, vg.reshape(DB, T, GW), state_gla, l, heads=HG, dk=GDK, dv=GDV,
                                 seqs_per_step=max(s for s in range(1, 9) if DB % s == 0))
        o_dec = _attn_decode(page_table, qd.reshape(DB, T, DQK), base, slope_rows, newbase,
                             kd.reshape(DB, T * HD, DK2), vd.reshape(DB, T * HD, DV), lamv[l],
                             cache_k2, cache_v2, l, pages_per_step=pages_per_step, page=PAGE, lam_init=lam_init,
                             heads=HD, dh=DH)
        o_diff = o_dec.reshape(NS, DW)
        hs, *proj_s = stage(hs, (o_gla.reshape(NS, GW), zg, o_diff, zd) + gains, l + 1, lam_init, NS, 0)
        sk.append(kd.reshape(DB, T, HD, DK2))
        sv.append(vd.reshape(DB, T, HD, DV))
        ss.append(s_s)
    y_prompt = _final_norm(hp.reshape(B, L, D), final_norm_g[None], skip=NM, chunk=LANES)
    y_sample = _final_norm(hs.reshape(1, NS, D), final_norm_g[None], skip=0, chunk=NS).reshape(DB, T, D)
    return (y_prompt, y_sample, jnp.stack(pk), jnp.stack(pv), jnp.stack(ps),
            jnp.stack(sk), jnp.stack(sv), jnp.stack(ss))
```

```python
import functools
import math

import jax
import jax.numpy as jnp
from jax import lax
from jax.experimental import pallas as pl
from jax.experimental.pallas import tpu as pltpu

RMS_EPS = 1e-6
NEG_INF = -1e30
GLA_GATE_NORM = 16.0
GLA_CHUNK = 64
LOG2E = 1.4426950408889634
LANES = 128
VMEM_LIMIT_BYTES = 56 * 1024 * 1024
BF16 = jnp.bfloat16
F32 = jnp.float32


def _cparams(*sem):
    return pltpu.CompilerParams(dimension_semantics=sem, vmem_limit_bytes=VMEM_LIMIT_BYTES)


def _dot(a, b):
    return jnp.dot(a, b, preferred_element_type=F32)


def _dot_nt(a, b):
    return lax.dot_general(a, b, (((1,), (1,)), ((), ())), preferred_element_type=F32)


def _dot_tn(a, b):
    return lax.dot_general(a, b, (((0,), (0,)), ((), ())), preferred_element_type=F32)


def _lambda_init(layer):
    return 0.8 - 0.6 * math.exp(-0.3 * layer)


def _lam_from(lamv, lam_init):
    t1 = jnp.sum(lamv[0:1, :] * lamv[1:2, :], axis=-1, keepdims=True)
    t2 = jnp.sum(lamv[2:3, :] * lamv[3:4, :], axis=-1, keepdims=True)
    return jnp.exp(t1) - jnp.exp(t2) + lam_init


def _rms(x):
    return x * lax.rsqrt(jnp.mean(x * x, axis=-1, keepdims=True) + RMS_EPS)


def _row_stage_kernel(*refs, mix, proj, widths, qg_scale, qd_scale, kv_heads, kv_stacked, dv_g, dv_d, diff_scale):
    refs = list(refs)
    x = refs.pop(0)[...]
    if mix:
        og_ref, zg_ref, od_ref, zd_ref, gg_ref, gd_ref, wo_ref = refs[:7]
        del refs[:7]
    if proj:
        g_ref, w_ref, wglr_ref, wup_ref, bg_ref = refs[:5]
        del refs[:5]
    if kv_stacked:
        del refs[:2]
    if mix:
        parts = []
        for o_ref, z_ref, gain_ref, dv, post in ((og_ref, zg_ref, gg_ref, dv_g, 1.0),
                                                  (od_ref, zd_ref, gd_ref, dv_d, diff_scale)):
            for h in range(o_ref.shape[1] // dv):
                c = slice(h * dv, (h + 1) * dv)
                z = z_ref[:, c]
                y = _rms(o_ref[:, c]) * gain_ref[...]
                if post != 1.0:
                    y = y * post
                parts.append(y * (z * jax.nn.sigmoid(z)))
        x = x + _dot(jnp.concatenate(parts, axis=-1).astype(BF16), wo_ref[...])
        refs.pop(0)[...] = x
    if proj:
        qg_ref, kg_ref, vg_ref, la_ref, zg_out, qd_ref, kd_ref, vd_ref, zd_out = refs
        u = (_rms(x) * g_ref[...]).astype(BF16)
        tm = x.shape[0]
        lo = 0
        for o_ref, wd in zip((qg_ref, kg_ref, vg_ref, zg_out, qd_ref, kd_ref, vd_ref, zd_out), widths):
            y = _dot(u, w_ref[:, lo:lo + wd])
            if o_ref is qg_ref:
                y = y * qg_scale
            if o_ref is qd_ref:
                y = y * qd_scale
            if kv_heads and (o_ref is kd_ref or o_ref is vd_ref):
                for h in range(kv_heads):
                    o_ref[pl.ds(h, tm, stride=kv_heads), :] = y[:, h * LANES:(h + 1) * LANES]
            else:
                o_ref[...] = y.astype(o_ref.dtype)
            lo += wd
        glr = _dot(u, wglr_ref[...]).astype(BF16)
        gate = _dot(glr, wup_ref[...]) + bg_ref[...]
        log_sig = jnp.minimum(gate, 0.0) - jnp.log1p(jnp.exp(-jnp.abs(gate)))
        la_ref[...] = log_sig / GLA_GATE_NORM


def _row_stage(h, mix_args, proj_args, kv_stack, *, widths, qg_scale, qd_scale, tm, kv_heads, dv_g, dv_d, diff_scale):
    n, d = h.shape
    row = lambda i: (i, 0)
    fixed = lambda i: (0, 0)
    rows = lambda a: pl.BlockSpec((tm, a.shape[1]), row)
    whole = lambda a: pl.BlockSpec(a.shape, fixed)
    args, in_specs, out_specs, out_shapes, aliases = [h], [rows(h)], [], [], {}
    if mix_args is not None:
        args += list(mix_args)
        in_specs += [rows(a) for a in mix_args[:4]] + [whole(a) for a in mix_args[4:]]
        out_specs.append(pl.BlockSpec((tm, d), row))
        out_shapes.append(jax.ShapeDtypeStruct((n, d), F32))
    if proj_args is not None:
        args += list(proj_args)
        in_specs += [whole(a) for a in proj_args]
        gqk, _, gw, _, dqk, _, dw, _ = widths
        for i, w in enumerate((gqk, gqk, gw, gqk, gw, dqk, dqk, dw, dw)):
            split = kv_heads if (kv_heads and i in (6, 7)) else 1
            if kv_stack is not None and i in (6, 7):
                stack, layer = kv_stack[i - 6], kv_stack[2]
                aliases[len(args)] = len(out_specs)
                args.append(stack)
                in_specs.append(pl.BlockSpec(memory_space=pl.ANY))
                out_specs.append(pl.BlockSpec((None, tm * split, w // split), lambda r, layer=layer: (layer, r, 0)))
                out_shapes.append(jax.ShapeDtypeStruct(stack.shape, stack.dtype))
            else:
                out_specs.append(pl.BlockSpec((tm * split, w // split), row))
                out_shapes.append(jax.ShapeDtypeStruct((n * split, w // split), BF16 if i == 5 else F32))
    return pl.pallas_call(
        functools.partial(_row_stage_kernel, mix=mix_args is not None, proj=proj_args is not None,
                          widths=widths, qg_scale=qg_scale, qd_scale=qd_scale, kv_heads=kv_heads,
                          kv_stacked=kv_stack is not None, dv_g=dv_g, dv_d=dv_d, diff_scale=diff_scale),
        grid=(n // tm,),
        in_specs=in_specs,
        out_specs=out_specs,
        out_shape=out_shapes,
        input_output_aliases=aliases,
        compiler_params=_cparams("parallel"),
        name="row_stage",
    )(*args)


def _gla_prompt_kernel(q_ref, k_ref, v_ref, la_ref, o_ref, s_ref, st_scr, *, n_meta, n_chunks, dk, dv, pairs):
    C = GLA_CHUNK
    st_scr[...] = jnp.zeros_like(st_scr)
    tril2 = (lax.broadcasted_iota(jnp.int32, (C, 2 * C), 0)
             >= lax.rem(lax.broadcasted_iota(jnp.int32, (C, 2 * C), 1), C))
    head_a = lax.broadcasted_iota(jnp.int32, (1, 2 * dk), 1) < dk
    val_a = lax.broadcasted_iota(jnp.int32, (1, 2 * dv), 1) < dv
    block_diag = ((lax.broadcasted_iota(jnp.int32, (2 * dv, 2 * dk), 0) < dv)
                  == (lax.broadcasted_iota(jnp.int32, (2 * dv, 2 * dk), 1) < dk))
    row = lax.broadcasted_iota(jnp.int32, (C, 1), 0)

    def cumsum_rows(x):
        shift = 1
        while shift < C:
            x = x + jnp.where(row >= shift, pltpu.roll(x, shift, axis=0), 0.0)
            shift *= 2
        return x

    def chunk(r0, n_valid):
        outs = []
        for p in range(pairs):
            qk_l = slice(p * 2 * dk, (p + 1) * 2 * dk)
            v_l = slice(p * 2 * dv, (p + 1) * 2 * dv)
            la = la_ref[pl.ds(r0, C), qk_l]
            q = q_ref[pl.ds(r0, C), qk_l]
            k = k_ref[pl.ds(r0, C), qk_l]
            v = v_ref[pl.ds(r0, C), v_l]
            if n_valid < C:
                keep = row < n_valid
                la = jnp.where(keep, la, 0.0)
                k = jnp.where(keep, k, 0.0)
                v = jnp.where(keep, v, 0.0)
            b = cumsum_rows(la)
            b_last = b[C - 1:C, :]
            q_in = (q * jnp.exp(b)).astype(BF16)
            k_in = (k * jnp.exp(-b)).astype(BF16)
            k_st = (k * jnp.exp(b_last - b)).astype(BF16)
            vb = v.astype(BF16)
            zk = jnp.zeros_like(k_in)
            k_bd = jnp.concatenate([jnp.where(head_a, k_in, zk), jnp.where(head_a, zk, k_in)], axis=0)
            att = jnp.where(tril2, _dot_nt(q_in, k_bd), 0.0).astype(BF16)
            zv = jnp.zeros_like(vb)
            v_bd = jnp.concatenate([jnp.where(val_a, vb, zv), jnp.where(val_a, zv, vb)], axis=0)
            o_intra = _dot(att, v_bd)
            st = st_scr[p]
            outs.append(o_intra + _dot_nt(q_in, st.astype(BF16)))
            d_st = jnp.where(block_diag, _dot_tn(vb, k_st), 0.0)
            st_scr[p] = st * jnp.exp(b_last) + d_st
        return jnp.concatenate(outs, axis=-1)

    o0 = chunk(0, n_meta)
    o_ref[0:n_meta, :] = o0[0:n_meta, :]

    def body(c, carry):
        r0 = pl.multiple_of(n_meta + c * C, 8)
        o_ref[pl.ds(r0, C), :] = chunk(r0, C)
        return carry

    lax.fori_loop(0, n_chunks, body, 0, unroll=4)
    for p in range(pairs):
        S = st_scr[p].T
        s_ref[2 * p] = S[0:dk, 0:dv]
        s_ref[2 * p + 1] = S[dk:2 * dk, dv:2 * dv]


def _gla_prompt(qg, kg, vg, la, *, batch, seq_len, n_meta, heads, dk, dv):
    n_chunks = (seq_len - n_meta) // GLA_CHUNK
    qk_spec = pl.BlockSpec((seq_len, heads * dk), lambda b: (b, 0))
    v_spec = pl.BlockSpec((seq_len, heads * dv), lambda b: (b, 0))
    return pl.pallas_call(
        functools.partial(_gla_prompt_kernel, n_meta=n_meta, n_chunks=n_chunks, dk=dk, dv=dv, pairs=heads // 2),
        grid=(batch,),
        in_specs=[qk_spec, qk_spec, v_spec, qk_spec],
        out_specs=[v_spec, pl.BlockSpec((None, heads, dk, dv), lambda b: (b, 0, 0, 0))],
        out_shape=[jax.ShapeDtypeStruct((batch * seq_len, heads * dv), F32),
                   jax.ShapeDtypeStruct((batch, heads, dk, dv), F32)],
        scratch_shapes=[pltpu.VMEM((heads // 2, 2 * dv, 2 * dk), F32)],
        compiler_params=_cparams("parallel"),
        name="gla_prompt",
    )(qg, kg, vg, la)


def _gla_decode_kernel(qklT_ref, v_ref, s0_ref, o_ref, s_ref, *, heads, dk, dv, steps):
    hk = heads * dk
    for b in range(qklT_ref.shape[0]):
        for h in range(heads):
            S = s0_ref[b, h]
            cols = slice(h * dv, (h + 1) * dv)
            col = lambda part, t: qklT_ref[b, part * hk + h * dk:part * hk + (h + 1) * dk, t:t + 1]
            for t in range(steps):
                S = S * jnp.exp(col(2, t)) + col(1, t) * v_ref[b, t:t + 1, cols]
                o_ref[b, t:t + 1, cols] = jnp.sum(col(0, t) * S, axis=0, keepdims=True)
            s_ref[b, h] = S


def _gla_decode(qklT, v, state, layer, *, heads, dk, dv, seqs_per_step):
    db, _, steps = qklT.shape
    sb = seqs_per_step
    return pl.pallas_call(
        functools.partial(_gla_decode_kernel, heads=heads, dk=dk, dv=dv, steps=steps),
        grid=(db // sb,),
        in_specs=[pl.BlockSpec((sb,) + qklT.shape[1:], lambda i: (i, 0, 0)),
                  pl.BlockSpec((sb, steps, heads * dv), lambda i: (i, 0, 0)),
                  pl.BlockSpec((None, sb, heads, dk, dv), lambda i: (layer, i, 0, 0, 0))],
        out_specs=[pl.BlockSpec((sb, steps, heads * dv), lambda i: (i, 0, 0)),
                   pl.BlockSpec((sb, heads, dk, dv), lambda i: (i, 0, 0, 0))],
        out_shape=[jax.ShapeDtypeStruct((db, steps, heads * dv), F32),
                   jax.ShapeDtypeStruct((db, heads, dk, dv), F32)],
        compiler_params=_cparams("parallel"),
        name="gla_decode",
    )(qklT, v, state)


def _attn_prompt_kernel(slopes_ref, lamv_ref, q_ref, k_ref, v_ref, o_ref,
                        kb_scr, vb_scr, qm_scr, m_scr, acc_scr, *, heads, n_q, tq, dh, lam_init):
    seq_len = q_ref.shape[0]
    pad_len = kb_scr.shape[1]
    dv = v_ref.shape[1]
    tail = seq_len - n_q * tq
    lam = _lam_from(lamv_ref[...], lam_init)
    first = lax.broadcasted_iota(jnp.int32, (1, LANES), 1) < dh
    for h in range(heads):
        kb_scr[h, 0:seq_len, :] = k_ref[pl.ds(h, seq_len, stride=heads), :].astype(BF16)
        vb_scr[h, 0:seq_len, 0:dv] = v_ref[pl.ds(h, seq_len, stride=heads), :].astype(BF16)
        vb_scr[h, 0:seq_len, dv:2 * dv] = jnp.ones((seq_len, dv), BF16)
        kb_scr[h, seq_len:pad_len, :] = jnp.zeros((pad_len - seq_len, LANES), BF16)
        vb_scr[h, seq_len:pad_len, :] = jnp.zeros((pad_len - seq_len, 2 * dv), BF16)

    def attend(q0, rows, n_full, diag_width):
        r2 = 2 * rows
        for h in range(heads):
            q = q_ref[pl.ds(q0, rows), h * LANES:(h + 1) * LANES]
            zero = jnp.zeros_like(q)
            qm_scr[h, 0:rows, :] = jnp.where(first, q, zero)
            qm_scr[h, rows:r2, :] = jnp.where(first, zero, q)
            m_scr[h, 0:r2, :] = jnp.full((r2, LANES), NEG_INF, F32)
            acc_scr[h, 0:r2, :] = jnp.zeros((r2, 2 * dv), F32)
        q_pos = q0 + lax.rem(lax.broadcasted_iota(jnp.int32, (r2, 1), 0), rows)

        def step(r0, width, causal):
            rel = (r0 + lax.broadcasted_iota(jnp.int32, (1, width), 1) - q0)
            for h in range(heads):
                s = (_dot_nt(qm_scr[h, 0:r2, :], kb_scr[h, pl.ds(r0, width), :])
                     + (slopes_ref[h] * LOG2E) * rel.astype(F32))
                if causal:
                    s = jnp.where(rel + q0 <= q_pos, s, NEG_INF)
                m_prev = m_scr[h, 0:r2, :]
                m_new = jnp.maximum(m_prev, jnp.max(s, axis=-1, keepdims=True))
                alpha = jnp.exp2(m_prev - m_new)
                p = jnp.exp2(s - jnp.concatenate([m_new] * (width // LANES), axis=-1)).astype(BF16)
                acc_scr[h, 0:r2, :] = (acc_scr[h, 0:r2, :] * jnp.concatenate([alpha, alpha], axis=-1)
                                       + _dot(p, vb_scr[h, pl.ds(r0, width), :]))
                m_scr[h, 0:r2, :] = m_new

        def pair(j, carry):
            step(pl.multiple_of(j * (2 * tq), 2 * tq), 2 * tq, False)
            return carry

        lax.fori_loop(0, n_full // 2, pair, 0)
        if isinstance(n_full, int):
            if n_full % 2:
                step((n_full - 1) * tq, tq, False)
        else:
            @pl.when(n_full % 2 == 1)
            def _():
                step(pl.multiple_of((n_full - 1) * tq, tq), tq, False)
        step(q0, diag_width, True)
        for h in range(heads):
            acc = acc_scr[h, 0:r2, :]
            o = acc[:, 0:dv] / acc[:, dv:2 * dv]
            o_ref[pl.ds(q0, rows), h * dv:(h + 1) * dv] = o[0:rows, :] - lam * o[rows:r2, :]

    def q_body(i, carry):
        attend(pl.multiple_of(i * tq, tq), tq, i, tq)
        return carry

    lax.fori_loop(0, n_q, q_body, 0)
    if tail:
        attend(n_q * tq, tail, n_q, LANES)


def _attn_prompt(slopes, lamv, qd, k_stack, v_stack, layer, *, batch, seq_len, heads, dh, dv, lam_init, tq):
    n_q = seq_len // tq
    pad_len = n_q * tq + LANES if seq_len % tq else seq_len
    assert seq_len - n_q * tq <= LANES
    tok_spec = pl.BlockSpec((seq_len, heads * dv), lambda b: (b, 0))
    row_spec = pl.BlockSpec((None, seq_len * heads, dv), lambda b: (layer, b, 0))
    return pl.pallas_call(
        functools.partial(_attn_prompt_kernel, heads=heads, n_q=n_q, tq=tq, dh=dh, lam_init=lam_init),
        grid=(batch,),
        in_specs=[pl.BlockSpec(memory_space=pltpu.SMEM), pl.BlockSpec(lamv.shape, lambda b: (0, 0)),
                  tok_spec, row_spec, row_spec],
        out_specs=tok_spec,
        out_shape=jax.ShapeDtypeStruct((batch * seq_len, heads * dv), F32),
        scratch_shapes=[pltpu.VMEM((heads, pad_len, 2 * dh), BF16), pltpu.VMEM((heads, pad_len, 2 * dv), BF16),
                        pltpu.VMEM((heads, 2 * tq, 2 * dh), BF16),
                        pltpu.VMEM((heads, 2 * tq, LANES), F32), pltpu.VMEM((heads, 2 * tq, 2 * dv), F32)],
        compiler_params=_cparams("parallel"),
        name="attn_prompt",
    )(slopes, lamv, qd, k_stack, v_stack)


def _attn_decode_kernel(pt_ref, q_ref, base_ref, slope_ref, newbase_ref, kn_ref, vn_ref, lamv_ref, *rest,
                        pages_per_step, page, past, lam_init, heads, dh):
    del pt_ref
    k_refs = rest[:pages_per_step]
    v_refs = rest[pages_per_step:2 * pages_per_step]
    o_ref, qm_scr, kn_scr, vn_scr, m_scr, l_scr, acc_scr = rest[2 * pages_per_step:]
    j = pl.program_id(1)
    steps = q_ref.shape[0]
    half = heads * steps

    @pl.when(j == 0)
    def _():
        m_scr[...] = jnp.full_like(m_scr, NEG_INF)
        l_scr[...] = jnp.zeros_like(l_scr)
        acc_scr[...] = jnp.zeros_like(acc_scr)
        first = lax.broadcasted_iota(jnp.int32, (1, 2 * dh), 1) < dh
        for h in range(heads):
            q = q_ref[:, h * 2 * dh:(h + 1) * 2 * dh].astype(F32)
            qm_scr[h * steps:(h + 1) * steps, :] = jnp.where(first, q, 0.0)
            qm_scr[half + h * steps:half + (h + 1) * steps, :] = jnp.where(first, 0.0, q)

    qm = qm_scr[...].astype(BF16)
    slope = slope_ref[...]

    def update(scores, value_refs):
        m_prev = m_scr[...]
        tile_max = functools.reduce(jnp.maximum, scores)
        m_new = jnp.maximum(m_prev, jnp.max(tile_max, axis=-1, keepdims=True))
        alpha = jnp.exp2(m_prev - m_new)
        rep = scores[0].shape[1] // LANES
        m_rep = jnp.concatenate([m_new] * rep, axis=-1)
        p_sum = None
        pv = None
        for s, v_ref in zip(scores, value_refs):
            p = jnp.exp2(s - m_rep)
            p_sum = p if p_sum is None else p_sum + p
            d = _dot(p.astype(BF16), v_ref[...].astype(BF16))
            pv = d if pv is None else pv + d
        l_scr[...] = alpha * l_scr[...] + jnp.sum(p_sum, axis=-1, keepdims=True)
        acc_scr[...] = alpha * acc_scr[...] + pv
        m_scr[...] = m_new

    scores = []
    for i in range(pages_per_step):
        pos0 = ((j * pages_per_step + i) * page - past).astype(F32)
        scores.append(_dot_nt(qm, k_refs[i][...].astype(BF16)) + (base_ref[...] + slope * pos0))
    update(scores, v_refs)

    @pl.when(j == pl.num_programs(1) - 1)
    def _():
        n_new = kn_ref.shape[0]
        kn_scr[...] = jnp.zeros_like(kn_scr)
        vn_scr[...] = jnp.zeros_like(vn_scr)
        kn_scr[0:n_new, :] = kn_ref[...]
        vn_scr[0:n_new, :] = vn_ref[...]
        update([_dot_nt(qm, kn_scr[...].astype(BF16)) + newbase_ref[...]], [vn_scr])
        lam = _lam_from(lamv_ref[...], lam_init)
        o = acc_scr[...] / l_scr[...]
        o = o[0:half, :] - lam * o[half:2 * half, :]
        dv = vn_ref.shape[1]
        for h in range(heads):
            o_ref[:, h * dv:(h + 1) * dv] = o[h * steps:(h + 1) * steps, :]


def _attn_decode(page_table, q_tok, base, slope_rows, newbase, k_new, v_new, lamv, cache_k, cache_v, layer,
                 *, pages_per_step, page, lam_init, heads, dh):
    db, n_pages = page_table.shape
    steps = q_tok.shape[1]
    rows = 2 * heads * steps
    page_rows, width = cache_k.shape[2:]
    fixed = lambda a: pl.BlockSpec(a.shape, lambda b, j, pt: (0,) * a.ndim)
    per_seq = lambda a: pl.BlockSpec((None,) + a.shape[1:], lambda b, j, pt: (b, 0, 0))

    def page_spec(i):
        return pl.BlockSpec((None, None, page_rows, width),
                            lambda b, j, pt: (layer, pt[b, j * pages_per_step + i], 0, 0))

    pages = [page_spec(i) for i in range(pages_per_step)]
    grid_spec = pltpu.PrefetchScalarGridSpec(
        num_scalar_prefetch=1,
        grid=(db, n_pages // pages_per_step),
        in_specs=[per_seq(q_tok), fixed(base), fixed(slope_rows), fixed(newbase),
                  per_seq(k_new), per_seq(v_new), fixed(lamv)] + pages + pages,
        out_specs=pl.BlockSpec((None, steps, heads * width), lambda b, j, pt: (b, 0, 0)),
        scratch_shapes=[pltpu.VMEM((rows, 2 * dh), F32), pltpu.VMEM((LANES, 2 * dh), F32),
                        pltpu.VMEM((LANES, width), F32),
                        pltpu.VMEM((rows, LANES), F32), pltpu.VMEM((rows, LANES), F32),
                        pltpu.VMEM((rows, width), F32)],
    )
    return pl.pallas_call(
        functools.partial(_attn_decode_kernel, pages_per_step=pages_per_step, page=page,
                          past=n_pages * page, lam_init=lam_init, heads=heads, dh=dh),
        grid_spec=grid_spec,
        out_shape=jax.ShapeDtypeStruct((db, steps, heads * width), F32),
        compiler_params=_cparams("parallel", "arbitrary"),
        name="attn_decode",
    )(page_table, q_tok, base, slope_rows, newbase, k_new, v_new, lamv,
      *([cache_k] * pages_per_step), *([cache_v] * pages_per_step))


def _final_norm_kernel(h_ref, g_ref, o_ref, *, skip, chunk):
    def body(i, carry):
        x = h_ref[pl.ds(pl.multiple_of(skip + i * chunk, 8), chunk), :]
        o_ref[pl.ds(pl.multiple_of(i * chunk, 8), chunk), :] = _rms(x) * g_ref[...]
        return carry

    lax.fori_loop(0, o_ref.shape[0] // chunk, body, 0)


def _final_norm(h3, g, *, skip, chunk):
    b, l, d = h3.shape
    return pl.pallas_call(
        functools.partial(_final_norm_kernel, skip=skip, chunk=chunk),
        grid=(b,),
        in_specs=[pl.BlockSpec((None, l, d), lambda i: (i, 0, 0)), pl.BlockSpec(g.shape, lambda i: (0, 0))],
        out_specs=pl.BlockSpec((None, l - skip, d), lambda i: (i, 0, 0)),
        out_shape=jax.ShapeDtypeStruct((b, l - skip, d), F32),
        compiler_params=_cparams("parallel"),
        name="final_norm",
    )(h3, g)


def _row_tile(n, cap):
    best = 8
    for t in range(8, cap + 1, 8):
        if n % t == 0:
            best = t
    return best


def kernel(x_prompt, x_sample, cache_k, cache_v, state_gla, page_table, meta_tokens, norm_g, w_in, gla_w_up, gla_b_gate, gla_norm_g, lambda_q1, lambda_k1, lambda_q2, lambda_k2, diff_norm_g, w_out, final_norm_g):
    B, SEQ, D = x_prompt.shape
    DB, T, _ = x_sample.shape
    DEPTH, NPOOL, PAGE, HD, DK2 = cache_k.shape
    DV = cache_v.shape[-1]
    DH = DK2 // 2
    NM = meta_tokens.shape[0]
    HG, GDK, GDV = state_gla.shape[2:]
    RANK = gla_w_up.shape[1]
    L = NM + SEQ
    NP, NS = B * L, DB * T
    GQK, GW, DQK, DW = HG * GDK, HG * GDV, HD * DK2, HD * DV
    widths = (GQK, GQK, GW, GW, DQK, DQK, DW, DW)
    assert w_in.shape[-1] == sum(widths) + RANK and DK2 == LANES and DV == LANES and 2 * GDK == LANES

    off = 2 * GQK + GW
    w_in_b = w_in.astype(BF16)
    w_main = jnp.concatenate([w_in_b[..., :off], w_in_b[..., off + RANK:]], axis=-1)
    w_glr = jnp.pad(w_in_b[..., off:off + RANK], ((0, 0), (0, 0), (0, LANES - RANK)))
    w_up = jnp.pad(gla_w_up, ((0, 0), (0, LANES - RANK), (0, 0))).astype(BF16)
    w_out_b = w_out.astype(BF16)
    lamv = jnp.stack([lambda_q1, lambda_k1, lambda_q2, lambda_k2], axis=1)
    slopes = 2.0 ** (-8.0 * jnp.arange(1, HD + 1, dtype=F32) / HD)
    cache_k2 = cache_k.reshape(DEPTH, NPOOL, PAGE * HD, DK2)
    cache_v2 = cache_v.reshape(DEPTH, NPOOL, PAGE * HD, DV)

    rows = 2 * HD * T
    row_h = (jnp.arange(rows) // T) % HD
    row_t = jnp.arange(rows) % T
    row_slope = (slopes[row_h] * LOG2E)[:, None]
    slope_rows = jnp.broadcast_to(row_slope, (rows, PAGE * HD))
    col = jnp.arange(PAGE * HD)
    base = jnp.where((col % HD)[None, :] == row_h[:, None], row_slope * (col // HD)[None, :].astype(F32), NEG_INF)
    ncol = jnp.arange(LANES)
    new_ok = (ncol[None, :] < T * HD) & ((ncol % HD)[None, :] == row_h[:, None]) & ((ncol // HD)[None, :] <= row_t[:, None])
    newbase = jnp.where(new_ok, row_slope * (ncol // HD)[None, :].astype(F32), NEG_INF)

    hp = jnp.concatenate([jnp.broadcast_to(meta_tokens[None], (B, NM, D)), x_prompt], axis=1).reshape(NP, D)
    hs = x_sample.reshape(NS, D)
    tm_p = _row_tile(NP, 512)
    n_pages = page_table.shape[1]
    pages_per_step = max(p for p in range(1, 33) if n_pages % p == 0)
    ps, sk, sv, ss = [], [], [], []

    def proj_params(l):
        return (norm_g[l][None], w_main[l], w_glr[l], w_up[l], gla_b_gate[l][None])

    def stage(h, mix_args, l_next, lam_init, tm, kv_stack):
        opening = l_next < DEPTH
        return _row_stage(h, mix_args, proj_params(l_next) if opening else None,
                          kv_stack + (l_next,) if (kv_stack and opening) else None, widths=widths,
                          qg_scale=GDK ** -0.5, qd_scale=DH ** -0.5 * LOG2E, tm=tm,
                          kv_heads=HD if kv_stack else 0, dv_g=GDV, dv_d=DV, diff_scale=1.0 - lam_init)

    kv = (jnp.zeros((DEPTH, NP * HD, DK2), F32), jnp.zeros((DEPTH, NP * HD, DV), F32))
    proj_p = stage(hp, None, 0, 0.0, tm_p, kv)
    proj_s = stage(hs, None, 0, 0.0, NS, None)
    for l in range(DEPTH):
        lam_init = _lambda_init(l)
        gains = (gla_norm_g[l][None], diff_norm_g[l][None], w_out_b[l])
        qg, kg, vg, la, zg, qd, k_stack, v_stack, zd = proj_p
        kv = (k_stack, v_stack)
        o_gla, s_p = _gla_prompt(qg, kg, vg, la, batch=B, seq_len=L, n_meta=NM, heads=HG, dk=GDK, dv=GDV)
        o_diff = _attn_prompt(slopes, lamv[l], qd, k_stack, v_stack, l, batch=B, seq_len=L, heads=HD,
                              dh=DH, dv=DV, lam_init=lam_init, tq=2 * LANES)
        hp, *proj_p = stage(hp, (o_gla, zg, o_diff, zd) + gains, l + 1, lam_init, tm_p, kv)
        ps.append(s_p)
        qg, kg, vg, la, zg, qd, kd, vd, zd = proj_s
        qklT = jnp.concatenate([qg, kg, la], axis=-1).reshape(DB, T, 3 * GQK).transpose(0, 2, 1)
        o_gla, s_s = _gla_decode(qklT, vg.reshape(DB, T, GW), state_gla, l, heads=HG, dk=GDK, dv=GDV,
                                 seqs_per_step=max(s for s in range(1, 9) if DB % s == 0))
        o_dec = _attn_decode(page_table, qd.reshape(DB, T, DQK), base, slope_rows, newbase,
                             kd.reshape(DB, T * HD, DK2), vd.reshape(DB, T * HD, DV), lamv[l],
                             cache_k2, cache_v2, l, pages_per_step=pages_per_step, page=PAGE, lam_init=lam_init,
                             heads=HD, dh=DH)
        o_diff = o_dec.reshape(NS, DW)
        hs, *proj_s = stage(hs, (o_gla.reshape(NS, GW), zg, o_diff, zd) + gains, l + 1, lam_init, NS, None)
        sk.append(kd.reshape(DB, T, HD, DK2))
        sv.append(vd.reshape(DB, T, HD, DV))
        ss.append(s_s)
    y_prompt = _final_norm(hp.reshape(B, L, D), final_norm_g[None], skip=NM, chunk=LANES)
    y_sample = _final_norm(hs.reshape(1, NS, D), final_norm_g[None], skip=0, chunk=NS).reshape(DB, T, D)
    pk, pv = (a.reshape(DEPTH, B, L, HD, a.shape[-1]) for a in kv)
    return (y_prompt, y_sample, pk, pv, jnp.stack(ps),
            jnp.stack(sk), jnp.stack(sv), jnp.stack(ss))
```

```python
import functools
import math

import jax
import jax.numpy as jnp
from jax import lax
from jax.experimental import pallas as pl
from jax.experimental.pallas import tpu as pltpu

RMS_EPS = 1e-6
NEG_INF = -1e30
GLA_GATE_NORM = 16.0
GLA_CHUNK = 64
LOG2E = 1.4426950408889634
LANES = 128
VMEM_LIMIT_BYTES = 56 * 1024 * 1024
ROW_TILE_CAP = 704
BF16 = jnp.bfloat16
F32 = jnp.float32


def _cparams(*sem):
    return pltpu.CompilerParams(dimension_semantics=sem, vmem_limit_bytes=VMEM_LIMIT_BYTES)


def _dot(a, b):
    return jnp.dot(a, b, preferred_element_type=F32)


def _dot_nt(a, b):
    return lax.dot_general(a, b, (((1,), (1,)), ((), ())), preferred_element_type=F32)


def _dot_tn(a, b):
    return lax.dot_general(a, b, (((0,), (0,)), ((), ())), preferred_element_type=F32)


def _lambda_init(layer):
    return 0.8 - 0.6 * math.exp(-0.3 * layer)


def _lam_from(lamv, lam_init):
    t1 = jnp.sum(lamv[0:1, :] * lamv[1:2, :], axis=-1, keepdims=True)
    t2 = jnp.sum(lamv[2:3, :] * lamv[3:4, :], axis=-1, keepdims=True)
    return jnp.exp(t1) - jnp.exp(t2) + lam_init


def _rms(x):
    return x * lax.rsqrt(jnp.mean(x * x, axis=-1, keepdims=True) + RMS_EPS)


def _prep_weights_kernel(w_ref, main_ref, glr_ref, *, off, rank):
    x = w_ref[...]
    main_ref[:, 0:off] = x[:, 0:off].astype(BF16)
    main_ref[:, off:] = x[:, off + rank:].astype(BF16)
    lane = lax.broadcasted_iota(jnp.int32, (1, LANES), 1)
    glr_ref[...] = jnp.where(lane < rank, x[:, off:off + LANES], 0.0).astype(BF16)


def _prep_weights(w_in, *, off, rank, tr):
    depth, d, cols = w_in.shape
    blk = lambda w: pl.BlockSpec((None, tr, w), lambda l, i: (l, i, 0))
    return pl.pallas_call(
        functools.partial(_prep_weights_kernel, off=off, rank=rank),
        grid=(depth, d // tr),
        in_specs=[blk(cols)],
        out_specs=[blk(cols - rank), blk(LANES)],
        out_shape=[jax.ShapeDtypeStruct((depth, d, cols - rank), BF16),
                   jax.ShapeDtypeStruct((depth, d, LANES), BF16)],
        compiler_params=_cparams("parallel", "parallel"),
        name="prep_weights",
    )(w_in)


def _row_stage_kernel(*refs, mix, proj, widths, qg_scale, qd_scale, kv_heads, kv_stacked, dv_g, dv_d, diff_scale):
    refs = list(refs)
    x = refs.pop(0)[...]
    if mix:
        og_ref, zg_ref, od_ref, zd_ref, gg_ref, gd_ref, wo_ref = refs[:7]
        del refs[:7]
    if proj:
        g_ref, w_ref, wglr_ref, wup_ref, bg_ref = refs[:5]
        del refs[:5]
    if kv_stacked:
        del refs[:2]
    if mix:
        parts = []
        for o_ref, z_ref, gain_ref, dv, post in ((og_ref, zg_ref, gg_ref, dv_g, 1.0),
                                                  (od_ref, zd_ref, gd_ref, dv_d, diff_scale)):
            for h in range(o_ref.shape[1] // dv):
                c = slice(h * dv, (h + 1) * dv)
                z = z_ref[:, c]
                y = _rms(o_ref[:, c]) * gain_ref[...]
                if post != 1.0:
                    y = y * post
                parts.append(y * (z * jax.nn.sigmoid(z)))
        x = x + _dot(jnp.concatenate(parts, axis=-1).astype(BF16), wo_ref[...])
        refs.pop(0)[...] = x
    if proj:
        qg_ref, kg_ref, vg_ref, la_ref, zg_out, qd_ref, kd_ref, vd_ref, zd_out = refs
        u = (_rms(x) * g_ref[...]).astype(BF16)
        tm = x.shape[0]
        lo = 0
        for o_ref, wd in zip((qg_ref, kg_ref, vg_ref, zg_out, qd_ref, kd_ref, vd_ref, zd_out), widths):
            y = _dot(u, w_ref[:, lo:lo + wd])
            if o_ref is qg_ref:
                y = y * qg_scale
            if o_ref is qd_ref:
                y = y * qd_scale
            if kv_heads and (o_ref is kd_ref or o_ref is vd_ref):
                for h in range(kv_heads):
                    o_ref[pl.ds(h, tm, stride=kv_heads), :] = y[:, h * LANES:(h + 1) * LANES]
            else:
                o_ref[...] = y.astype(o_ref.dtype)
            lo += wd
        glr = _dot(u, wglr_ref[...]).astype(BF16)
        gate = _dot(glr, wup_ref[...]) + bg_ref[...]
        log_sig = jnp.minimum(gate, 0.0) - jnp.log1p(jnp.exp(-jnp.abs(gate)))
        la_ref[...] = log_sig / GLA_GATE_NORM


def _row_stage(h, mix_args, proj_args, kv_stack, *, widths, qg_scale, qd_scale, tm, kv_heads, dv_g, dv_d, diff_scale):
    n, d = h.shape
    row = lambda i: (i, 0)
    fixed = lambda i: (0, 0)
    rows = lambda a: pl.BlockSpec((tm, a.shape[1]), row)
    whole = lambda a: pl.BlockSpec(a.shape, fixed)
    args, in_specs, out_specs, out_shapes, aliases = [h], [rows(h)], [], [], {}
    if mix_args is not None:
        args += list(mix_args)
        in_specs += [rows(a) for a in mix_args[:4]] + [whole(a) for a in mix_args[4:]]
        out_specs.append(pl.BlockSpec((tm, d), row))
        out_shapes.append(jax.ShapeDtypeStruct((n, d), F32))
    if proj_args is not None:
        args += list(proj_args)
        in_specs += [whole(a) for a in proj_args]
        gqk, _, gw, _, dqk, _, dw, _ = widths
        for i, w in enumerate((gqk, gqk, gw, gqk, gw, dqk, dqk, dw, dw)):
            split = kv_heads if (kv_heads and i in (6, 7)) else 1
            if kv_stack is not None and i in (6, 7):
                stack, layer = kv_stack[i - 6], kv_stack[2]
                aliases[len(args)] = len(out_specs)
                args.append(stack)
                in_specs.append(pl.BlockSpec(memory_space=pl.ANY))
                out_specs.append(pl.BlockSpec((None, tm * split, w // split), lambda r, layer=layer: (layer, r, 0)))
                out_shapes.append(jax.ShapeDtypeStruct(stack.shape, stack.dtype))
            else:
                out_specs.append(pl.BlockSpec((tm * split, w // split), row))
                out_shapes.append(jax.ShapeDtypeStruct((n * split, w // split), BF16 if i == 5 else F32))
    return pl.pallas_call(
        functools.partial(_row_stage_kernel, mix=mix_args is not None, proj=proj_args is not None,
                          widths=widths, qg_scale=qg_scale, qd_scale=qd_scale, kv_heads=kv_heads,
                          kv_stacked=kv_stack is not None, dv_g=dv_g, dv_d=dv_d, diff_scale=diff_scale),
        grid=(n // tm,),
        in_specs=in_specs,
        out_specs=out_specs,
        out_shape=out_shapes,
        input_output_aliases=aliases,
        compiler_params=_cparams("parallel"),
        name="row_stage",
    )(*args)


def _gla_prompt_kernel(q_ref, k_ref, v_ref, la_ref, o_ref, s_ref, st_scr, *, n_meta, n_chunks, dk, dv, pairs):
    C = GLA_CHUNK
    st_scr[...] = jnp.zeros_like(st_scr)
    tril2 = (lax.broadcasted_iota(jnp.int32, (C, 2 * C), 0)
             >= lax.rem(lax.broadcasted_iota(jnp.int32, (C, 2 * C), 1), C))
    head_a = lax.broadcasted_iota(jnp.int32, (1, 2 * dk), 1) < dk
    val_a = lax.broadcasted_iota(jnp.int32, (1, 2 * dv), 1) < dv
    block_diag = ((lax.broadcasted_iota(jnp.int32, (2 * dv, 2 * dk), 0) < dv)
                  == (lax.broadcasted_iota(jnp.int32, (2 * dv, 2 * dk), 1) < dk))
    row = lax.broadcasted_iota(jnp.int32, (C, 1), 0)

    def cumsum_rows(x):
        shift = 1
        while shift < C:
            x = x + jnp.where(row >= shift, pltpu.roll(x, shift, axis=0), 0.0)
            shift *= 2
        return x

    def chunk(r0, n_valid):
        outs = []
        for p in range(pairs):
            qk_l = slice(p * 2 * dk, (p + 1) * 2 * dk)
            v_l = slice(p * 2 * dv, (p + 1) * 2 * dv)
            la = la_ref[pl.ds(r0, C), qk_l]
            q = q_ref[pl.ds(r0, C), qk_l]
            k = k_ref[pl.ds(r0, C), qk_l]
            v = v_ref[pl.ds(r0, C), v_l]
            if n_valid < C:
                keep = row < n_valid
                la = jnp.where(keep, la, 0.0)
                k = jnp.where(keep, k, 0.0)
                v = jnp.where(keep, v, 0.0)
            b = cumsum_rows(la)
            b_last = b[C - 1:C, :]
            q_in = (q * jnp.exp(b)).astype(BF16)
            k_in = (k * jnp.exp(-b)).astype(BF16)
            k_st = (k * jnp.exp(b_last - b)).astype(BF16)
            vb = v.astype(BF16)
            zk = jnp.zeros_like(k_in)
            k_bd = jnp.concatenate([jnp.where(head_a, k_in, zk), jnp.where(head_a, zk, k_in)], axis=0)
            att = jnp.where(tril2, _dot_nt(q_in, k_bd), 0.0).astype(BF16)
            zv = jnp.zeros_like(vb)
            v_bd = jnp.concatenate([jnp.where(val_a, vb, zv), jnp.where(val_a, zv, vb)], axis=0)
            o_intra = _dot(att, v_bd)
            st = st_scr[p]
            outs.append(o_intra + _dot_nt(q_in, st.astype(BF16)))
            d_st = jnp.where(block_diag, _dot_tn(vb, k_st), 0.0)
            st_scr[p] = st * jnp.exp(b_last) + d_st
        return jnp.concatenate(outs, axis=-1)

    o0 = chunk(0, n_meta)
    o_ref[0:n_meta, :] = o0[0:n_meta, :]

    def body(c, carry):
        r0 = pl.multiple_of(n_meta + c * C, 8)
        o_ref[pl.ds(r0, C), :] = chunk(r0, C)
        return carry

    lax.fori_loop(0, n_chunks, body, 0, unroll=4)
    for p in range(pairs):
        S = st_scr[p].T
        s_ref[2 * p] = S[0:dk, 0:dv]
        s_ref[2 * p + 1] = S[dk:2 * dk, dv:2 * dv]


def _gla_prompt(qg, kg, vg, la, *, batch, seq_len, n_meta, heads, dk, dv):
    n_chunks = (seq_len - n_meta) // GLA_CHUNK
    qk_spec = pl.BlockSpec((seq_len, heads * dk), lambda b: (b, 0))
    v_spec = pl.BlockSpec((seq_len, heads * dv), lambda b: (b, 0))
    return pl.pallas_call(
        functools.partial(_gla_prompt_kernel, n_meta=n_meta, n_chunks=n_chunks, dk=dk, dv=dv, pairs=heads // 2),
        grid=(batch,),
        in_specs=[qk_spec, qk_spec, v_spec, qk_spec],
        out_specs=[v_spec, pl.BlockSpec((None, heads, dk, dv), lambda b: (b, 0, 0, 0))],
        out_shape=[jax.ShapeDtypeStruct((batch * seq_len, heads * dv), F32),
                   jax.ShapeDtypeStruct((batch, heads, dk, dv), F32)],
        scratch_shapes=[pltpu.VMEM((heads // 2, 2 * dv, 2 * dk), F32)],
        compiler_params=_cparams("parallel"),
        name="gla_prompt",
    )(qg, kg, vg, la)


def _gla_decode_kernel(qklT_ref, v_ref, s0_ref, o_ref, s_ref, *, heads, dk, dv, steps):
    hk = heads * dk
    for b in range(qklT_ref.shape[0]):
        for h in range(heads):
            S = s0_ref[b, h]
            cols = slice(h * dv, (h + 1) * dv)
            col = lambda part, t: qklT_ref[b, part * hk + h * dk:part * hk + (h + 1) * dk, t:t + 1]
            for t in range(steps):
                S = S * jnp.exp(col(2, t)) + col(1, t) * v_ref[b, t:t + 1, cols]
                o_ref[b, t:t + 1, cols] = jnp.sum(col(0, t) * S, axis=0, keepdims=True)
            s_ref[b, h] = S


def _gla_decode(qklT, v, state, layer, *, heads, dk, dv, seqs_per_step):
    db, _, steps = qklT.shape
    sb = seqs_per_step
    return pl.pallas_call(
        functools.partial(_gla_decode_kernel, heads=heads, dk=dk, dv=dv, steps=steps),
        grid=(db // sb,),
        in_specs=[pl.BlockSpec((sb,) + qklT.shape[1:], lambda i: (i, 0, 0)),
                  pl.BlockSpec((sb, steps, heads * dv), lambda i: (i, 0, 0)),
                  pl.BlockSpec((None, sb, heads, dk, dv), lambda i: (layer, i, 0, 0, 0))],
        out_specs=[pl.BlockSpec((sb, steps, heads * dv), lambda i: (i, 0, 0)),
                   pl.BlockSpec((sb, heads, dk, dv), lambda i: (i, 0, 0, 0))],
        out_shape=[jax.ShapeDtypeStruct((db, steps, heads * dv), F32),
                   jax.ShapeDtypeStruct((db, heads, dk, dv), F32)],
        compiler_params=_cparams("parallel"),
        name="gla_decode",
    )(qklT, v, state)


def _attn_prompt_kernel(slopes_ref, lamv_ref, q_ref, k_ref, v_ref, o_ref,
                        kb_scr, vb_scr, qm_scr, m_scr, acc_scr, *, heads, n_q, tq, dh, lam_init):
    seq_len = q_ref.shape[0]
    pad_len = kb_scr.shape[1]
    dv = v_ref.shape[1]
    tail = seq_len - n_q * tq
    lam = _lam_from(lamv_ref[...], lam_init)
    first = lax.broadcasted_iota(jnp.int32, (1, LANES), 1) < dh
    for h in range(heads):
        kb_scr[h, 0:seq_len, :] = k_ref[pl.ds(h, seq_len, stride=heads), :].astype(BF16)
        vb_scr[h, 0:seq_len, 0:dv] = v_ref[pl.ds(h, seq_len, stride=heads), :].astype(BF16)
        vb_scr[h, 0:seq_len, dv:2 * dv] = jnp.ones((seq_len, dv), BF16)
        kb_scr[h, seq_len:pad_len, :] = jnp.zeros((pad_len - seq_len, LANES), BF16)
        vb_scr[h, seq_len:pad_len, :] = jnp.zeros((pad_len - seq_len, 2 * dv), BF16)

    wide = 2 * tq

    def attend(tiles, n_wide):
        for t, (q0, rows, _, _) in enumerate(tiles):
            r2 = 2 * rows
            for h in range(heads):
                q = q_ref[pl.ds(q0, rows), h * LANES:(h + 1) * LANES]
                zero = jnp.zeros_like(q)
                qm_scr[t, h, 0:rows, :] = jnp.where(first, q, zero)
                qm_scr[t, h, rows:r2, :] = jnp.where(first, zero, q)
                m_scr[t, h, 0:r2, :] = jnp.full((r2, LANES), NEG_INF, F32)
                acc_scr[t, h, 0:r2, :] = jnp.zeros((r2, 2 * dv), F32)

        def step(t, r0, width, causal):
            q0, rows, _, _ = tiles[t]
            r2 = 2 * rows
            rel = (r0 + lax.broadcasted_iota(jnp.int32, (1, width), 1) - q0)
            for h in range(heads):
                s = (_dot_nt(qm_scr[t, h, 0:r2, :], kb_scr[h, pl.ds(r0, width), :])
                     + (slopes_ref[h] * LOG2E) * rel.astype(F32))
                if causal:
                    row_pos = lax.rem(lax.broadcasted_iota(jnp.int32, (r2, 1), 0), rows)
                    s = jnp.where(rel <= row_pos, s, NEG_INF)
                m_prev = m_scr[t, h, 0:r2, :]
                m_new = jnp.maximum(m_prev, jnp.max(s, axis=-1, keepdims=True))
                alpha = jnp.exp2(m_prev - m_new)
                p = jnp.exp2(s - jnp.concatenate([m_new] * (width // LANES), axis=-1)).astype(BF16)
                acc_scr[t, h, 0:r2, :] = (acc_scr[t, h, 0:r2, :] * jnp.concatenate([alpha, alpha], axis=-1)
                                          + _dot(p, vb_scr[h, pl.ds(r0, width), :]))
                m_scr[t, h, 0:r2, :] = m_new

        def wide_step(j, carry):
            for t in range(len(tiles)):
                step(t, pl.multiple_of(j * wide, wide), wide, False)
            return carry

        lax.fori_loop(0, n_wide, wide_step, 0)
        for t, (_, _, last_r0, last_width) in enumerate(tiles):
            step(t, last_r0, last_width, True)
        for t, (q0, rows, _, _) in enumerate(tiles):
            r2 = 2 * rows
            for h in range(heads):
                acc = acc_scr[t, h, 0:r2, :]
                o = acc[:, 0:dv] / acc[:, dv:2 * dv]
                o_ref[pl.ds(q0, rows), h * dv:(h + 1) * dv] = o[0:rows, :] - lam * o[rows:r2, :]

    def pair_body(i, carry):
        q0 = pl.multiple_of(i * wide, wide)
        attend([(q0, tq, q0, tq), (q0 + tq, tq, q0, wide)], i)
        return carry

    lax.fori_loop(0, n_q // 2, pair_body, 0)
    if n_q % 2:
        attend([((n_q - 1) * tq, tq, (n_q - 1) * tq, tq)], (n_q - 1) // 2)
    if tail:
        last_r0 = (n_q - n_q % 2) * tq
        attend([(n_q * tq, tail, last_r0, n_q * tq + LANES - last_r0)], n_q // 2)


def _attn_prompt(slopes, lamv, qd, k_stack, v_stack, layer, *, batch, seq_len, heads, dh, dv, lam_init, tq):
    n_q = seq_len // tq
    pad_len = n_q * tq + LANES if seq_len % tq else seq_len
    assert seq_len - n_q * tq <= LANES
    tok_spec = pl.BlockSpec((seq_len, heads * dv), lambda b: (b, 0))
    row_spec = pl.BlockSpec((None, seq_len * heads, dv), lambda b: (layer, b, 0))
    return pl.pallas_call(
        functools.partial(_attn_prompt_kernel, heads=heads, n_q=n_q, tq=tq, dh=dh, lam_init=lam_init),
        grid=(batch,),
        in_specs=[pl.BlockSpec(memory_space=pltpu.SMEM), pl.BlockSpec(lamv.shape, lambda b: (0, 0)),
                  tok_spec, row_spec, row_spec],
        out_specs=tok_spec,
        out_shape=jax.ShapeDtypeStruct((batch * seq_len, heads * dv), F32),
        scratch_shapes=[pltpu.VMEM((heads, pad_len, 2 * dh), BF16), pltpu.VMEM((heads, pad_len, 2 * dv), BF16),
                        pltpu.VMEM((2, heads, 2 * tq, 2 * dh), BF16),
                        pltpu.VMEM((2, heads, 2 * tq, LANES), F32), pltpu.VMEM((2, heads, 2 * tq, 2 * dv), F32)],
        compiler_params=_cparams("parallel"),
        name="attn_prompt",
    )(slopes, lamv, qd, k_stack, v_stack)


def _attn_decode_kernel(pt_ref, q_ref, base_ref, slope_ref, newbase_ref, kn_ref, vn_ref, lamv_ref, *rest,
                        pages_per_step, page, past, lam_init, heads, dh):
    del pt_ref
    k_refs = rest[:pages_per_step]
    v_refs = rest[pages_per_step:2 * pages_per_step]
    o_ref, qm_scr, kn_scr, vn_scr, m_scr, l_scr, acc_scr = rest[2 * pages_per_step:]
    j = pl.program_id(1)
    steps = q_ref.shape[0]
    half = heads * steps

    @pl.when(j == 0)
    def _():
        m_scr[...] = jnp.full_like(m_scr, NEG_INF)
        l_scr[...] = jnp.zeros_like(l_scr)
        acc_scr[...] = jnp.zeros_like(acc_scr)
        first = lax.broadcasted_iota(jnp.int32, (1, 2 * dh), 1) < dh
        for h in range(heads):
            q = q_ref[:, h * 2 * dh:(h + 1) * 2 * dh].astype(F32)
            qm_scr[h * steps:(h + 1) * steps, :] = jnp.where(first, q, 0.0)
            qm_scr[half + h * steps:half + (h + 1) * steps, :] = jnp.where(first, 0.0, q)

    qm = qm_scr[...].astype(BF16)
    slope = slope_ref[...]

    def update(scores, value_refs):
        m_prev = m_scr[...]
        tile_max = functools.reduce(jnp.maximum, scores)
        m_new = jnp.maximum(m_prev, jnp.max(tile_max, axis=-1, keepdims=True))
        alpha = jnp.exp2(m_prev - m_new)
        rep = scores[0].shape[1] // LANES
        m_rep = jnp.concatenate([m_new] * rep, axis=-1)
        p_sum = None
        pv = None
        for s, v_ref in zip(scores, value_refs):
            p = jnp.exp2(s - m_rep)
            p_sum = p if p_sum is None else p_sum + p
            d = _dot(p.astype(BF16), v_ref[...].astype(BF16))
            pv = d if pv is None else pv + d
        l_scr[...] = alpha * l_scr[...] + jnp.sum(p_sum, axis=-1, keepdims=True)
        acc_scr[...] = alpha * acc_scr[...] + pv
        m_scr[...] = m_new

    scores = []
    for i in range(pages_per_step):
        pos0 = ((j * pages_per_step + i) * page - past).astype(F32)
        scores.append(_dot_nt(qm, k_refs[i][...].astype(BF16)) + (base_ref[...] + slope * pos0))
    update(scores, v_refs)

    @pl.when(j == pl.num_programs(1) - 1)
    def _():
        n_new = kn_ref.shape[0]
        kn_scr[...] = jnp.zeros_like(kn_scr)
        vn_scr[...] = jnp.zeros_like(vn_scr)
        kn_scr[0:n_new, :] = kn_ref[...]
        vn_scr[0:n_new, :] = vn_ref[...]
        update([_dot_nt(qm, kn_scr[...].astype(BF16)) + newbase_ref[...]], [vn_scr])
        lam = _lam_from(lamv_ref[...], lam_init)
        o = acc_scr[...] / l_scr[...]
        o = o[0:half, :] - lam * o[half:2 * half, :]
        dv = vn_ref.shape[1]
        for h in range(heads):
            o_ref[:, h * dv:(h + 1) * dv] = o[h * steps:(h + 1) * steps, :]


def _attn_decode(page_table, q_tok, base, slope_rows, newbase, k_new, v_new, lamv, cache_k, cache_v, layer,
                 *, pages_per_step, page, lam_init, heads, dh):
    db, n_pages = page_table.shape
    steps = q_tok.shape[1]
    rows = 2 * heads * steps
    page_rows, width = cache_k.shape[2:]
    fixed = lambda a: pl.BlockSpec(a.shape, lambda b, j, pt: (0,) * a.ndim)
    per_seq = lambda a: pl.BlockSpec((None,) + a.shape[1:], lambda b, j, pt: (b, 0, 0))

    def page_spec(i):
        return pl.BlockSpec((None, None, page_rows, width),
                            lambda b, j, pt: (layer, pt[b, j * pages_per_step + i], 0, 0))

    pages = [page_spec(i) for i in range(pages_per_step)]
    grid_spec = pltpu.PrefetchScalarGridSpec(
        num_scalar_prefetch=1,
        grid=(db, n_pages // pages_per_step),
        in_specs=[per_seq(q_tok), fixed(base), fixed(slope_rows), fixed(newbase),
                  per_seq(k_new), per_seq(v_new), fixed(lamv)] + pages + pages,
        out_specs=pl.BlockSpec((None, steps, heads * width), lambda b, j, pt: (b, 0, 0)),
        scratch_shapes=[pltpu.VMEM((rows, 2 * dh), F32), pltpu.VMEM((LANES, 2 * dh), F32),
                        pltpu.VMEM((LANES, width), F32),
                        pltpu.VMEM((rows, LANES), F32), pltpu.VMEM((rows, LANES), F32),
                        pltpu.VMEM((rows, width), F32)],
    )
    return pl.pallas_call(
        functools.partial(_attn_decode_kernel, pages_per_step=pages_per_step, page=page,
                          past=n_pages * page, lam_init=lam_init, heads=heads, dh=dh),
        grid_spec=grid_spec,
        out_shape=jax.ShapeDtypeStruct((db, steps, heads * width), F32),
        compiler_params=_cparams("parallel", "arbitrary"),
        name="attn_decode",
    )(page_table, q_tok, base, slope_rows, newbase, k_new, v_new, lamv,
      *([cache_k] * pages_per_step), *([cache_v] * pages_per_step))


def _final_norm_kernel(h_ref, g_ref, o_ref, *, skip, chunk):
    def body(i, carry):
        x = h_ref[pl.ds(pl.multiple_of(skip + i * chunk, 8), chunk), :]
        o_ref[pl.ds(pl.multiple_of(i * chunk, 8), chunk), :] = _rms(x) * g_ref[...]
        return carry

    lax.fori_loop(0, o_ref.shape[0] // chunk, body, 0)


def _final_norm(h3, g, *, skip, chunk):
    b, l, d = h3.shape
    return pl.pallas_call(
        functools.partial(_final_norm_kernel, skip=skip, chunk=chunk),
        grid=(b,),
        in_specs=[pl.BlockSpec((None, l, d), lambda i: (i, 0, 0)), pl.BlockSpec(g.shape, lambda i: (0, 0))],
        out_specs=pl.BlockSpec((None, l - skip, d), lambda i: (i, 0, 0)),
        out_shape=jax.ShapeDtypeStruct((b, l - skip, d), F32),
        compiler_params=_cparams("parallel"),
        name="final_norm",
    )(h3, g)


def _row_tile(n, cap):
    best = 8
    for t in range(8, cap + 1, 8):
        if n % t == 0:
            best = t
    return best


def kernel(x_prompt, x_sample, cache_k, cache_v, state_gla, page_table, meta_tokens, norm_g, w_in, gla_w_up, gla_b_gate, gla_norm_g, lambda_q1, lambda_k1, lambda_q2, lambda_k2, diff_norm_g, w_out, final_norm_g):
    B, SEQ, D = x_prompt.shape
    DB, T, _ = x_sample.shape
    DEPTH, NPOOL, PAGE, HD, DK2 = cache_k.shape
    DV = cache_v.shape[-1]
    DH = DK2 // 2
    NM = meta_tokens.shape[0]
    HG, GDK, GDV = state_gla.shape[2:]
    RANK = gla_w_up.shape[1]
    L = NM + SEQ
    NP, NS = B * L, DB * T
    GQK, GW, DQK, DW = HG * GDK, HG * GDV, HD * DK2, HD * DV
    widths = (GQK, GQK, GW, GW, DQK, DQK, DW, DW)
    assert w_in.shape[-1] == sum(widths) + RANK and DK2 == LANES and DV == LANES and 2 * GDK == LANES

    off = 2 * GQK + GW
    w_main, w_glr = _prep_weights(w_in, off=off, rank=RANK, tr=_row_tile(D, 256))
    w_up = jnp.pad(gla_w_up, ((0, 0), (0, LANES - RANK), (0, 0))).astype(BF16)
    w_out_b = w_out.astype(BF16)
    lamv = jnp.stack([lambda_q1, lambda_k1, lambda_q2, lambda_k2], axis=1)
    slopes = 2.0 ** (-8.0 * jnp.arange(1, HD + 1, dtype=F32) / HD)
    cache_k2 = cache_k.reshape(DEPTH, NPOOL, PAGE * HD, DK2)
    cache_v2 = cache_v.reshape(DEPTH, NPOOL, PAGE * HD, DV)

    rows = 2 * HD * T
    row_h = (jnp.arange(rows) // T) % HD
    row_t = jnp.arange(rows) % T
    row_slope = (slopes[row_h] * LOG2E)[:, None]
    slope_rows = jnp.broadcast_to(row_slope, (rows, PAGE * HD))
    col = jnp.arange(PAGE * HD)
    base = jnp.where((col % HD)[None, :] == row_h[:, None], row_slope * (col // HD)[None, :].astype(F32), NEG_INF)
    ncol = jnp.arange(LANES)
    new_ok = (ncol[None, :] < T * HD) & ((ncol % HD)[None, :] == row_h[:, None]) & ((ncol // HD)[None, :] <= row_t[:, None])
    newbase = jnp.where(new_ok, row_slope * (ncol // HD)[None, :].astype(F32), NEG_INF)

    hp = jnp.concatenate([jnp.broadcast_to(meta_tokens[None], (B, NM, D)), x_prompt], axis=1).reshape(NP, D)
    hs = x_sample.reshape(NS, D)
    tm_p = _row_tile(NP, ROW_TILE_CAP)
    n_pages = page_table.shape[1]
    pages_per_step = max(p for p in range(1, 33) if n_pages % p == 0)
    ps, sk, sv, ss = [], [], [], []

    def proj_params(l):
        return (norm_g[l][None], w_main[l], w_glr[l], w_up[l], gla_b_gate[l][None])

    def stage(h, mix_args, l_next, lam_init, tm, kv_stack):
        opening = l_next < DEPTH
        return _row_stage(h, mix_args, proj_params(l_next) if opening else None,
                          kv_stack + (l_next,) if (kv_stack and opening) else None, widths=widths,
                          qg_scale=GDK ** -0.5, qd_scale=DH ** -0.5 * LOG2E, tm=tm,
                          kv_heads=HD if kv_stack else 0, dv_g=GDV, dv_d=DV, diff_scale=1.0 - lam_init)

    kv = (jnp.zeros((DEPTH, NP * HD, DK2), F32), jnp.zeros((DEPTH, NP * HD, DV), F32))
    proj_p = stage(hp, None, 0, 0.0, tm_p, kv)
    proj_s = stage(hs, None, 0, 0.0, NS, None)
    for l in range(DEPTH):
        lam_init = _lambda_init(l)
        gains = (gla_norm_g[l][None], diff_norm_g[l][None], w_out_b[l])
        qg, kg, vg, la, zg, qd, k_stack, v_stack, zd = proj_p
        kv = (k_stack, v_stack)
        o_gla, s_p = _gla_prompt(qg, kg, vg, la, batch=B, seq_len=L, n_meta=NM, heads=HG, dk=GDK, dv=GDV)
        o_diff = _attn_prompt(slopes, lamv[l], qd, k_stack, v_stack, l, batch=B, seq_len=L, heads=HD,
                              dh=DH, dv=DV, lam_init=lam_init, tq=2 * LANES)
        hp, *proj_p = stage(hp, (o_gla, zg, o_diff, zd) + gains, l + 1, lam_init, tm_p, kv)
        ps.append(s_p)
        qg, kg, vg, la, zg, qd, kd, vd, zd = proj_s
        qklT = jnp.concatenate([qg, kg, la], axis=-1).reshape(DB, T, 3 * GQK).transpose(0, 2, 1)
        o_gla, s_s = _gla_decode(qklT, vg.reshape(DB, T, GW), state_gla, l, heads=HG, dk=GDK, dv=GDV,
                                 seqs_per_step=max(s for s in range(1, 9) if DB % s == 0))
        o_dec = _attn_decode(page_table, qd.reshape(DB, T, DQK), base, slope_rows, newbase,
                             kd.reshape(DB, T * HD, DK2), vd.reshape(DB, T * HD, DV), lamv[l],
                             cache_k2, cache_v2, l, pages_per_step=pages_per_step, page=PAGE, lam_init=lam_init,
                             heads=HD, dh=DH)
        o_diff = o_dec.reshape(NS, DW)
        hs, *proj_s = stage(hs, (o_gla.reshape(NS, GW), zg, o_diff, zd) + gains, l + 1, lam_init, NS, None)
        sk.append(kd.reshape(DB, T, HD, DK2))
        sv.append(vd.reshape(DB, T, HD, DV))
        ss.append(s_s)
    y_prompt = _final_norm(hp.reshape(B, L, D), final_norm_g[None], skip=NM, chunk=LANES)
    y_sample = _final_norm(hs.reshape(1, NS, D), final_norm_g[None], skip=0, chunk=NS).reshape(DB, T, D)
    pk, pv = (a.reshape(DEPTH, B, L, HD, a.shape[-1]) for a in kv)
    return (y_prompt, y_sample, pk, pv, jnp.stack(ps),
            jnp.stack(sk), jnp.stack(sv), jnp.stack(ss))
```

```python
import functools
import math

import jax
import jax.numpy as jnp
from jax import lax
from jax.experimental import pallas as pl
from jax.experimental.pallas import tpu as pltpu

RMS_EPS = 1e-6
NEG_INF = -1e30
GLA_GATE_NORM = 16.0
GLA_CHUNK = 64
LOG2E = 1.4426950408889634
LANES = 128
VMEM_LIMIT_BYTES = 56 * 1024 * 1024
ROW_TILE_CAP = 704
BF16 = jnp.bfloat16
F32 = jnp.float32


def _cparams(*sem):
    return pltpu.CompilerParams(dimension_semantics=sem, vmem_limit_bytes=VMEM_LIMIT_BYTES)


def _dot(a, b):
    return jnp.dot(a, b, preferred_element_type=F32)


def _dot_nt(a, b):
    return lax.dot_general(a, b, (((1,), (1,)), ((), ())), preferred_element_type=F32)


def _dot_tn(a, b):
    return lax.dot_general(a, b, (((0,), (0,)), ((), ())), preferred_element_type=F32)


def _lambda_init(layer):
    return 0.8 - 0.6 * math.exp(-0.3 * layer)


def _lam_from(lamv, lam_init):
    t1 = jnp.sum(lamv[0:1, :] * lamv[1:2, :], axis=-1, keepdims=True)
    t2 = jnp.sum(lamv[2:3, :] * lamv[3:4, :], axis=-1, keepdims=True)
    return jnp.exp(t1) - jnp.exp(t2) + lam_init


def _rms(x):
    return x * lax.rsqrt(jnp.mean(x * x, axis=-1, keepdims=True) + RMS_EPS)


def _row_stage_kernel(*refs, mix, proj, widths, gate_rank, qg_scale, qd_scale, kv_heads, kv_stacked, dv_g, dv_d,
                      diff_scale):
    refs = list(refs)
    x = refs.pop(0)[...]
    if mix:
        og_ref, zg_ref, od_ref, zd_ref, gg_ref, gd_ref, wo_ref = refs[:7]
        del refs[:7]
    if proj:
        g_ref, wt_ref, wup_ref, bg_ref = refs[:4]
        del refs[:4]
    if kv_stacked:
        del refs[:2]
    if mix:
        parts = []
        for o_ref, z_ref, gain_ref, dv, post in ((og_ref, zg_ref, gg_ref, dv_g, 1.0),
                                                  (od_ref, zd_ref, gd_ref, dv_d, diff_scale)):
            for h in range(o_ref.shape[1] // dv):
                c = slice(h * dv, (h + 1) * dv)
                z = z_ref[:, c]
                y = _rms(o_ref[:, c]) * gain_ref[...]
                if post != 1.0:
                    y = y * post
                parts.append(y * (z * jax.nn.sigmoid(z)))
        x = x + _dot(jnp.concatenate(parts, axis=-1).astype(BF16), wo_ref[...])
        refs.pop(0)[...] = x
    if proj:
        qg_ref, kg_ref, vg_ref, la_ref, zg_out, qd_ref, kd_ref, vd_ref, zd_out = refs
        u = (_rms(x) * g_ref[...]).astype(BF16)
        tm = x.shape[0]
        lo = 0
        for o_ref, wd in zip((qg_ref, kg_ref, vg_ref, zg_out, qd_ref, kd_ref, vd_ref, zd_out), widths):
            if o_ref is zg_out:
                lo += gate_rank
            y = _dot_nt(u, wt_ref[lo:lo + wd, :])
            if o_ref is qg_ref:
                y = y * qg_scale
            if o_ref is qd_ref:
                y = y * qd_scale
            if kv_heads and (o_ref is kd_ref or o_ref is vd_ref):
                for h in range(kv_heads):
                    o_ref[pl.ds(h, tm, stride=kv_heads), :] = y[:, h * LANES:(h + 1) * LANES]
            else:
                o_ref[...] = y.astype(o_ref.dtype)
            lo += wd
        gate_lo = widths[0] + widths[1] + widths[2]
        glr = _dot_nt(u, wt_ref[gate_lo:gate_lo + LANES, :]).astype(BF16)
        gate = _dot(glr, wup_ref[...]) + bg_ref[...]
        log_sig = jnp.minimum(gate, 0.0) - jnp.log1p(jnp.exp(-jnp.abs(gate)))
        la_ref[...] = log_sig / GLA_GATE_NORM


def _row_stage(h, mix_args, mix_layer, proj_args, proj_layer, kv_stack, *, widths, gate_rank, qg_scale, qd_scale, tm,
               kv_heads, dv_g, dv_d, diff_scale):
    n, d = h.shape
    row = lambda i: (i, 0)
    rows = lambda a: pl.BlockSpec((tm, a.shape[1]), row)
    layered = lambda a, layer: pl.BlockSpec((None,) + a.shape[1:], lambda i: (layer, 0, 0))
    args, in_specs, out_specs, out_shapes, aliases = [h], [rows(h)], [], [], {}
    if mix_args is not None:
        args += list(mix_args)
        in_specs += [rows(a) for a in mix_args[:4]] + [layered(a, mix_layer) for a in mix_args[4:]]
        out_specs.append(pl.BlockSpec((tm, d), row))
        out_shapes.append(jax.ShapeDtypeStruct((n, d), F32))
    if proj_args is not None:
        args += list(proj_args)
        in_specs += [layered(a, proj_layer) for a in proj_args]
        gqk, _, gw, _, dqk, _, dw, _ = widths
        for i, w in enumerate((gqk, gqk, gw, gqk, gw, dqk, dqk, dw, dw)):
            split = kv_heads if (kv_heads and i in (6, 7)) else 1
            if kv_stack is not None and i in (6, 7):
                stack = kv_stack[i - 6]
                aliases[len(args)] = len(out_specs)
                args.append(stack)
                in_specs.append(pl.BlockSpec(memory_space=pl.ANY))
                out_specs.append(pl.BlockSpec((None, tm * split, w // split), lambda r: (proj_layer, r, 0)))
                out_shapes.append(jax.ShapeDtypeStruct(stack.shape, stack.dtype))
            else:
                out_specs.append(pl.BlockSpec((tm * split, w // split), row))
                out_shapes.append(jax.ShapeDtypeStruct((n * split, w // split), BF16 if i == 5 else F32))
    return pl.pallas_call(
        functools.partial(_row_stage_kernel, mix=mix_args is not None, proj=proj_args is not None,
                          widths=widths, gate_rank=gate_rank, qg_scale=qg_scale, qd_scale=qd_scale,
                          kv_heads=kv_heads,
                          kv_stacked=kv_stack is not None, dv_g=dv_g, dv_d=dv_d, diff_scale=diff_scale),
        grid=(n // tm,),
        in_specs=in_specs,
        out_specs=out_specs,
        out_shape=out_shapes,
        input_output_aliases=aliases,
        compiler_params=_cparams("parallel"),
        name="row_stage",
    )(*args)


def _gla_prompt_kernel(q_ref, k_ref, v_ref, la_ref, o_ref, s_ref, st_scr, *, n_meta, n_chunks, dk, dv, pairs):
    C = GLA_CHUNK
    st_scr[...] = jnp.zeros_like(st_scr)
    tril2 = (lax.broadcasted_iota(jnp.int32, (C, 2 * C), 0)
             >= lax.rem(lax.broadcasted_iota(jnp.int32, (C, 2 * C), 1), C))
    head_a = lax.broadcasted_iota(jnp.int32, (1, 2 * dk), 1) < dk
    val_a = lax.broadcasted_iota(jnp.int32, (1, 2 * dv), 1) < dv
    block_diag = ((lax.broadcasted_iota(jnp.int32, (2 * dv, 2 * dk), 0) < dv)
                  == (lax.broadcasted_iota(jnp.int32, (2 * dv, 2 * dk), 1) < dk))
    row = lax.broadcasted_iota(jnp.int32, (C, 1), 0)

    def cumsum_rows(x):
        shift = 1
        while shift < C:
            x = x + jnp.where(row >= shift, pltpu.roll(x, shift, axis=0), 0.0)
            shift *= 2
        return x

    def chunk(r0, n_valid):
        outs = []
        for p in range(pairs):
            qk_l = slice(p * 2 * dk, (p + 1) * 2 * dk)
            v_l = slice(p * 2 * dv, (p + 1) * 2 * dv)
            la = la_ref[pl.ds(r0, C), qk_l]
            q = q_ref[pl.ds(r0, C), qk_l]
            k = k_ref[pl.ds(r0, C), qk_l]
            v = v_ref[pl.ds(r0, C), v_l]
            if n_valid < C:
                keep = row < n_valid
                la = jnp.where(keep, la, 0.0)
                k = jnp.where(keep, k, 0.0)
                v = jnp.where(keep, v, 0.0)
            b = cumsum_rows(la)
            b_last = b[C - 1:C, :]
            q_in = (q * jnp.exp(b)).astype(BF16)
            k_in = (k * jnp.exp(-b)).astype(BF16)
            k_st = (k * jnp.exp(b_last - b)).astype(BF16)
            vb = v.astype(BF16)
            zk = jnp.zeros_like(k_in)
            k_bd = jnp.concatenate([jnp.where(head_a, k_in, zk), jnp.where(head_a, zk, k_in)], axis=0)
            att = jnp.where(tril2, _dot_nt(q_in, k_bd), 0.0).astype(BF16)
            zv = jnp.zeros_like(vb)
            v_bd = jnp.concatenate([jnp.where(val_a, vb, zv), jnp.where(val_a, zv, vb)], axis=0)
            o_intra = _dot(att, v_bd)
            st = st_scr[p]
            outs.append(o_intra + _dot_nt(q_in, st.astype(BF16)))
            d_st = jnp.where(block_diag, _dot_tn(vb, k_st), 0.0)
            st_scr[p] = st * jnp.exp(b_last) + d_st
        return jnp.concatenate(outs, axis=-1)

    o0 = chunk(0, n_meta)
    o_ref[0:n_meta, :] = o0[0:n_meta, :]

    def body(c, carry):
        r0 = pl.multiple_of(n_meta + c * C, 8)
        o_ref[pl.ds(r0, C), :] = chunk(r0, C)
        return carry

    lax.fori_loop(0, n_chunks, body, 0, unroll=4)
    for p in range(pairs):
        S = st_scr[p].T
        s_ref[2 * p] = S[0:dk, 0:dv]
        s_ref[2 * p + 1] = S[dk:2 * dk, dv:2 * dv]


def _gla_prompt(qg, kg, vg, la, *, batch, seq_len, n_meta, heads, dk, dv):
    n_chunks = (seq_len - n_meta) // GLA_CHUNK
    qk_spec = pl.BlockSpec((seq_len, heads * dk), lambda b: (b, 0))
    v_spec = pl.BlockSpec((seq_len, heads * dv), lambda b: (b, 0))
    return pl.pallas_call(
        functools.partial(_gla_prompt_kernel, n_meta=n_meta, n_chunks=n_chunks, dk=dk, dv=dv, pairs=heads // 2),
        grid=(batch,),
        in_specs=[qk_spec, qk_spec, v_spec, qk_spec],
        out_specs=[v_spec, pl.BlockSpec((None, heads, dk, dv), lambda b: (b, 0, 0, 0))],
        out_shape=[jax.ShapeDtypeStruct((batch * seq_len, heads * dv), F32),
                   jax.ShapeDtypeStruct((batch, heads, dk, dv), F32)],
        scratch_shapes=[pltpu.VMEM((heads // 2, 2 * dv, 2 * dk), F32)],
        compiler_params=_cparams("parallel"),
        name="gla_prompt",
    )(qg, kg, vg, la)


def _gla_decode_kernel(qklT_ref, v_ref, s0_ref, o_ref, s_ref, *, heads, dk, dv, steps):
    hk = heads * dk
    for b in range(qklT_ref.shape[0]):
        for h in range(heads):
            S = s0_ref[b, h]
            cols = slice(h * dv, (h + 1) * dv)
            col = lambda part, t: qklT_ref[b, part * hk + h * dk:part * hk + (h + 1) * dk, t:t + 1]
            for t in range(steps):
                S = S * jnp.exp(col(2, t)) + col(1, t) * v_ref[b, t:t + 1, cols]
                o_ref[b, t:t + 1, cols] = jnp.sum(col(0, t) * S, axis=0, keepdims=True)
            s_ref[b, h] = S


def _gla_decode(qklT, v, state, layer, *, heads, dk, dv, seqs_per_step):
    db, _, steps = qklT.shape
    sb = seqs_per_step
    return pl.pallas_call(
        functools.partial(_gla_decode_kernel, heads=heads, dk=dk, dv=dv, steps=steps),
        grid=(db // sb,),
        in_specs=[pl.BlockSpec((sb,) + qklT.shape[1:], lambda i: (i, 0, 0)),
                  pl.BlockSpec((sb, steps, heads * dv), lambda i: (i, 0, 0)),
                  pl.BlockSpec((None, sb, heads, dk, dv), lambda i: (layer, i, 0, 0, 0))],
        out_specs=[pl.BlockSpec((sb, steps, heads * dv), lambda i: (i, 0, 0)),
                   pl.BlockSpec((sb, heads, dk, dv), lambda i: (i, 0, 0, 0))],
        out_shape=[jax.ShapeDtypeStruct((db, steps, heads * dv), F32),
                   jax.ShapeDtypeStruct((db, heads, dk, dv), F32)],
        compiler_params=_cparams("parallel"),
        name="gla_decode",
    )(qklT, v, state)


def _attn_prompt_kernel(slopes_ref, lamv_ref, q_ref, k_ref, v_ref, o_ref,
                        kb_scr, vb_scr, qm_scr, m_scr, acc_scr, *, heads, n_q, tq, dh, lam_init):
    seq_len = q_ref.shape[0]
    pad_len = kb_scr.shape[1]
    dv = v_ref.shape[1]
    tail = seq_len - n_q * tq
    lam = _lam_from(lamv_ref[...], lam_init)
    first = lax.broadcasted_iota(jnp.int32, (1, LANES), 1) < dh
    for h in range(heads):
        kb_scr[h, 0:seq_len, :] = k_ref[pl.ds(h, seq_len, stride=heads), :].astype(BF16)
        vb_scr[h, 0:seq_len, 0:dv] = v_ref[pl.ds(h, seq_len, stride=heads), :].astype(BF16)
        vb_scr[h, 0:seq_len, dv:2 * dv] = jnp.ones((seq_len, dv), BF16)
        kb_scr[h, seq_len:pad_len, :] = jnp.zeros((pad_len - seq_len, LANES), BF16)
        vb_scr[h, seq_len:pad_len, :] = jnp.zeros((pad_len - seq_len, 2 * dv), BF16)

    wide = 2 * tq

    def attend(tiles, n_wide):
        for t, (q0, rows, _, _) in enumerate(tiles):
            r2 = 2 * rows
            for h in range(heads):
                q = q_ref[pl.ds(q0, rows), h * LANES:(h + 1) * LANES]
                zero = jnp.zeros_like(q)
                qm_scr[t, h, 0:rows, :] = jnp.where(first, q, zero)
                qm_scr[t, h, rows:r2, :] = jnp.where(first, zero, q)
                m_scr[t, h, 0:r2, :] = jnp.full((r2, LANES), NEG_INF, F32)
                acc_scr[t, h, 0:r2, :] = jnp.zeros((r2, 2 * dv), F32)

        def step(t, r0, width, causal):
            q0, rows, _, _ = tiles[t]
            r2 = 2 * rows
            rel = (r0 + lax.broadcasted_iota(jnp.int32, (1, width), 1) - q0)
            for h in range(heads):
                s = (_dot_nt(qm_scr[t, h, 0:r2, :], kb_scr[h, pl.ds(r0, width), :])
                     + (slopes_ref[h] * LOG2E) * rel.astype(F32))
                if causal:
                    row_pos = lax.rem(lax.broadcasted_iota(jnp.int32, (r2, 1), 0), rows)
                    s = jnp.where(rel <= row_pos, s, NEG_INF)
                m_prev = m_scr[t, h, 0:r2, :]
                m_new = jnp.maximum(m_prev, jnp.max(s, axis=-1, keepdims=True))
                alpha = jnp.exp2(m_prev - m_new)
                p = jnp.exp2(s - jnp.concatenate([m_new] * (width // LANES), axis=-1)).astype(BF16)
                acc_scr[t, h, 0:r2, :] = (acc_scr[t, h, 0:r2, :] * jnp.concatenate([alpha, alpha], axis=-1)
                                          + _dot(p, vb_scr[h, pl.ds(r0, width), :]))
                m_scr[t, h, 0:r2, :] = m_new

        def wide_step(j, carry):
            for t in range(len(tiles)):
                step(t, pl.multiple_of(j * wide, wide), wide, False)
            return carry

        lax.fori_loop(0, n_wide, wide_step, 0)
        for t, (_, _, last_r0, last_width) in enumerate(tiles):
            step(t, last_r0, last_width, True)
        for t, (q0, rows, _, _) in enumerate(tiles):
            r2 = 2 * rows
            for h in range(heads):
                acc = acc_scr[t, h, 0:r2, :]
                o = acc[:, 0:dv] / acc[:, dv:2 * dv]
                o_ref[pl.ds(q0, rows), h * dv:(h + 1) * dv] = o[0:rows, :] - lam * o[rows:r2, :]

    def pair_body(i, carry):
        q0 = pl.multiple_of(i * wide, wide)
        attend([(q0, tq, q0, tq), (q0 + tq, tq, q0, wide)], i)
        return carry

    lax.fori_loop(0, n_q // 2, pair_body, 0)
    if n_q % 2:
        attend([((n_q - 1) * tq, tq, (n_q - 1) * tq, tq)], (n_q - 1) // 2)
    if tail:
        last_r0 = (n_q - n_q % 2) * tq
        attend([(n_q * tq, tail, last_r0, n_q * tq + LANES - last_r0)], n_q // 2)


def _attn_prompt(slopes, lamv, qd, k_stack, v_stack, layer, *, batch, seq_len, heads, dh, dv, lam_init, tq):
    n_q = seq_len // tq
    pad_len = n_q * tq + LANES if seq_len % tq else seq_len
    assert seq_len - n_q * tq <= LANES
    tok_spec = pl.BlockSpec((seq_len, heads * dv), lambda b: (b, 0))
    row_spec = pl.BlockSpec((None, seq_len * heads, dv), lambda b: (layer, b, 0))
    return pl.pallas_call(
        functools.partial(_attn_prompt_kernel, heads=heads, n_q=n_q, tq=tq, dh=dh, lam_init=lam_init),
        grid=(batch,),
        in_specs=[pl.BlockSpec(memory_space=pltpu.SMEM), pl.BlockSpec(lamv.shape, lambda b: (0, 0)),
                  tok_spec, row_spec, row_spec],
        out_specs=tok_spec,
        out_shape=jax.ShapeDtypeStruct((batch * seq_len, heads * dv), F32),
        scratch_shapes=[pltpu.VMEM((heads, pad_len, 2 * dh), BF16), pltpu.VMEM((heads, pad_len, 2 * dv), BF16),
                        pltpu.VMEM((2, heads, 2 * tq, 2 * dh), BF16),
                        pltpu.VMEM((2, heads, 2 * tq, LANES), F32), pltpu.VMEM((2, heads, 2 * tq, 2 * dv), F32)],
        compiler_params=_cparams("parallel"),
        name="attn_prompt",
    )(slopes, lamv, qd, k_stack, v_stack)


def _attn_decode_kernel(pt_ref, q_ref, base_ref, slope_ref, newbase_ref, kn_ref, vn_ref, lamv_ref, *rest,
                        pages_per_step, page, past, lam_init, heads, dh):
    del pt_ref
    k_refs = rest[:pages_per_step]
    v_refs = rest[pages_per_step:2 * pages_per_step]
    o_ref, qm_scr, kn_scr, vn_scr, m_scr, l_scr, acc_scr = rest[2 * pages_per_step:]
    j = pl.program_id(1)
    steps = q_ref.shape[0]
    half = heads * steps

    @pl.when(j == 0)
    def _():
        m_scr[...] = jnp.full_like(m_scr, NEG_INF)
        l_scr[...] = jnp.zeros_like(l_scr)
        acc_scr[...] = jnp.zeros_like(acc_scr)
        first = lax.broadcasted_iota(jnp.int32, (1, 2 * dh), 1) < dh
        for h in range(heads):
            q = q_ref[:, h * 2 * dh:(h + 1) * 2 * dh].astype(F32)
            qm_scr[h * steps:(h + 1) * steps, :] = jnp.where(first, q, 0.0)
            qm_scr[half + h * steps:half + (h + 1) * steps, :] = jnp.where(first, 0.0, q)

    qm = qm_scr[...].astype(BF16)
    slope = slope_ref[...]

    def update(scores, value_refs):
        m_prev = m_scr[...]
        tile_max = functools.reduce(jnp.maximum, scores)
        m_new = jnp.maximum(m_prev, jnp.max(tile_max, axis=-1, keepdims=True))
        alpha = jnp.exp2(m_prev - m_new)
        rep = scores[0].shape[1] // LANES
        m_rep = jnp.concatenate([m_new] * rep, axis=-1)
        p_sum = None
        pv = None
        for s, v_ref in zip(scores, value_refs):
            p = jnp.exp2(s - m_rep)
            p_sum = p if p_sum is None else p_sum + p
            d = _dot(p.astype(BF16), v_ref[...].astype(BF16))
            pv = d if pv is None else pv + d
        l_scr[...] = alpha * l_scr[...] + jnp.sum(p_sum, axis=-1, keepdims=True)
        acc_scr[...] = alpha * acc_scr[...] + pv
        m_scr[...] = m_new

    scores = []
    for i in range(pages_per_step):
        pos0 = ((j * pages_per_step + i) * page - past).astype(F32)
        scores.append(_dot_nt(qm, k_refs[i][...].astype(BF16)) + (base_ref[...] + slope * pos0))
    update(scores, v_refs)

    @pl.when(j == pl.num_programs(1) - 1)
    def _():
        n_new = kn_ref.shape[0]
        kn_scr[...] = jnp.zeros_like(kn_scr)
        vn_scr[...] = jnp.zeros_like(vn_scr)
        kn_scr[0:n_new, :] = kn_ref[...]
        vn_scr[0:n_new, :] = vn_ref[...]
        update([_dot_nt(qm, kn_scr[...].astype(BF16)) + newbase_ref[...]], [vn_scr])
        lam = _lam_from(lamv_ref[...], lam_init)
        o = acc_scr[...] / l_scr[...]
        o = o[0:half, :] - lam * o[half:2 * half, :]
        dv = vn_ref.shape[1]
        for h in range(heads):
            o_ref[:, h * dv:(h + 1) * dv] = o[h * steps:(h + 1) * steps, :]


def _attn_decode(page_table, q_tok, base, slope_rows, newbase, k_new, v_new, lamv, cache_k, cache_v, layer,
                 *, pages_per_step, page, lam_init, heads, dh):
    db, n_pages = page_table.shape
    steps = q_tok.shape[1]
    rows = 2 * heads * steps
    page_rows, width = cache_k.shape[2:]
    fixed = lambda a: pl.BlockSpec(a.shape, lambda b, j, pt: (0,) * a.ndim)
    per_seq = lambda a: pl.BlockSpec((None,) + a.shape[1:], lambda b, j, pt: (b, 0, 0))

    def page_spec(i):
        return pl.BlockSpec((None, None, page_rows, width),
                            lambda b, j, pt: (layer, pt[b, j * pages_per_step + i], 0, 0))

    pages = [page_spec(i) for i in range(pages_per_step)]
    grid_spec = pltpu.PrefetchScalarGridSpec(
        num_scalar_prefetch=1,
        grid=(db, n_pages // pages_per_step),
        in_specs=[per_seq(q_tok), fixed(base), fixed(slope_rows), fixed(newbase),
                  per_seq(k_new), per_seq(v_new), fixed(lamv)] + pages + pages,
        out_specs=pl.BlockSpec((None, steps, heads * width), lambda b, j, pt: (b, 0, 0)),
        scratch_shapes=[pltpu.VMEM((rows, 2 * dh), F32), pltpu.VMEM((LANES, 2 * dh), F32),
                        pltpu.VMEM((LANES, width), F32),
                        pltpu.VMEM((rows, LANES), F32), pltpu.VMEM((rows, LANES), F32),
                        pltpu.VMEM((rows, width), F32)],
    )
    return pl.pallas_call(
        functools.partial(_attn_decode_kernel, pages_per_step=pages_per_step, page=page,
                          past=n_pages * page, lam_init=lam_init, heads=heads, dh=dh),
        grid_spec=grid_spec,
        out_shape=jax.ShapeDtypeStruct((db, steps, heads * width), F32),
        compiler_params=_cparams("parallel", "arbitrary"),
        name="attn_decode",
    )(page_table, q_tok, base, slope_rows, newbase, k_new, v_new, lamv,
      *([cache_k] * pages_per_step), *([cache_v] * pages_per_step))


def _final_norm_kernel(h_ref, g_ref, o_ref, *, skip, chunk):
    def body(i, carry):
        x = h_ref[pl.ds(pl.multiple_of(skip + i * chunk, 8), chunk), :]
        o_ref[pl.ds(pl.multiple_of(i * chunk, 8), chunk), :] = _rms(x) * g_ref[...]
        return carry

    lax.fori_loop(0, o_ref.shape[0] // chunk, body, 0)


def _final_norm(h3, g, *, skip, chunk):
    b, l, d = h3.shape
    return pl.pallas_call(
        functools.partial(_final_norm_kernel, skip=skip, chunk=chunk),
        grid=(b,),
        in_specs=[pl.BlockSpec((None, l, d), lambda i: (i, 0, 0)), pl.BlockSpec(g.shape, lambda i: (0, 0))],
        out_specs=pl.BlockSpec((None, l - skip, d), lambda i: (i, 0, 0)),
        out_shape=jax.ShapeDtypeStruct((b, l - skip, d), F32),
        compiler_params=_cparams("parallel"),
        name="final_norm",
    )(h3, g)


def _row_tile(n, cap):
    best = 8
    for t in range(8, cap + 1, 8):
        if n % t == 0:
            best = t
    return best


def kernel(x_prompt, x_sample, cache_k, cache_v, state_gla, page_table, meta_tokens, norm_g, w_in, gla_w_up, gla_b_gate, gla_norm_g, lambda_q1, lambda_k1, lambda_q2, lambda_k2, diff_norm_g, w_out, final_norm_g):
    B, SEQ, D = x_prompt.shape
    DB, T, _ = x_sample.shape
    DEPTH, NPOOL, PAGE, HD, DK2 = cache_k.shape
    DV = cache_v.shape[-1]
    DH = DK2 // 2
    NM = meta_tokens.shape[0]
    HG, GDK, GDV = state_gla.shape[2:]
    RANK = gla_w_up.shape[1]
    L = NM + SEQ
    NP, NS = B * L, DB * T
    GQK, GW, DQK, DW = HG * GDK, HG * GDV, HD * DK2, HD * DV
    widths = (GQK, GQK, GW, GW, DQK, DQK, DW, DW)
    assert w_in.shape[-1] == sum(widths) + RANK and DK2 == LANES and DV == LANES and 2 * GDK == LANES

    w_in_t = jnp.swapaxes(w_in, 1, 2).astype(BF16)
    w_up = jnp.pad(gla_w_up, ((0, 0), (0, LANES - RANK), (0, 0))).astype(BF16)
    w_out_b = w_out.astype(BF16)
    proj_stacks = (norm_g[:, None, :], w_in_t, w_up, gla_b_gate[:, None, :])
    gain_stacks = (gla_norm_g[:, None, :], diff_norm_g[:, None, :], w_out_b)
    lamv = jnp.stack([lambda_q1, lambda_k1, lambda_q2, lambda_k2], axis=1)
    slopes = 2.0 ** (-8.0 * jnp.arange(1, HD + 1, dtype=F32) / HD)
    cache_k2 = cache_k.reshape(DEPTH, NPOOL, PAGE * HD, DK2)
    cache_v2 = cache_v.reshape(DEPTH, NPOOL, PAGE * HD, DV)

    rows = 2 * HD * T
    row_h = (jnp.arange(rows) // T) % HD
    row_t = jnp.arange(rows) % T
    row_slope = (slopes[row_h] * LOG2E)[:, None]
    slope_rows = jnp.broadcast_to(row_slope, (rows, PAGE * HD))
    col = jnp.arange(PAGE * HD)
    base = jnp.where((col % HD)[None, :] == row_h[:, None], row_slope * (col // HD)[None, :].astype(F32), NEG_INF)
    ncol = jnp.arange(LANES)
    new_ok = (ncol[None, :] < T * HD) & ((ncol % HD)[None, :] == row_h[:, None]) & ((ncol // HD)[None, :] <= row_t[:, None])
    newbase = jnp.where(new_ok, row_slope * (ncol // HD)[None, :].astype(F32), NEG_INF)

    hp = jnp.concatenate([jnp.broadcast_to(meta_tokens[None], (B, NM, D)), x_prompt], axis=1).reshape(NP, D)
    hs = x_sample.reshape(NS, D)
    tm_p = _row_tile(NP, ROW_TILE_CAP)
    n_pages = page_table.shape[1]
    pages_per_step = max(p for p in range(1, 33) if n_pages % p == 0)
    ps, sk, sv, ss = [], [], [], []

    def stage(h, mixed, l_next, lam_init, tm, kv_stack):
        opening = l_next < DEPTH
        return _row_stage(h, mixed + gain_stacks if mixed else None, l_next - 1,
                          proj_stacks if opening else None, l_next, kv_stack if opening else None,
                          widths=widths, gate_rank=RANK, qg_scale=GDK ** -0.5, qd_scale=DH ** -0.5 * LOG2E, tm=tm,
                          kv_heads=HD if kv_stack else 0, dv_g=GDV, dv_d=DV, diff_scale=1.0 - lam_init)

    kv = (jnp.zeros((DEPTH, NP * HD, DK2), F32), jnp.zeros((DEPTH, NP * HD, DV), F32))
    proj_p = stage(hp, None, 0, 0.0, tm_p, kv)
    proj_s = stage(hs, None, 0, 0.0, NS, None)
    for l in range(DEPTH):
        lam_init = _lambda_init(l)
        qg, kg, vg, la, zg, qd, k_stack, v_stack, zd = proj_p
        kv = (k_stack, v_stack)
        o_gla, s_p = _gla_prompt(qg, kg, vg, la, batch=B, seq_len=L, n_meta=NM, heads=HG, dk=GDK, dv=GDV)
        o_diff = _attn_prompt(slopes, lamv[l], qd, k_stack, v_stack, l, batch=B, seq_len=L, heads=HD,
                              dh=DH, dv=DV, lam_init=lam_init, tq=2 * LANES)
        hp, *proj_p = stage(hp, (o_gla, zg, o_diff, zd), l + 1, lam_init, tm_p, kv)
        ps.append(s_p)
        qg, kg, vg, la, zg, qd, kd, vd, zd = proj_s
        qklT = jnp.concatenate([qg, kg, la], axis=-1).reshape(DB, T, 3 * GQK).transpose(0, 2, 1)
        o_gla, s_s = _gla_decode(qklT, vg.reshape(DB, T, GW), state_gla, l, heads=HG, dk=GDK, dv=GDV,
                                 seqs_per_step=max(s for s in range(1, 9) if DB % s == 0))
        o_dec = _attn_decode(page_table, qd.reshape(DB, T, DQK), base, slope_rows, newbase,
                             kd.reshape(DB, T * HD, DK2), vd.reshape(DB, T * HD, DV), lamv[l],
                             cache_k2, cache_v2, l, pages_per_step=pages_per_step, page=PAGE, lam_init=lam_init,
                             heads=HD, dh=DH)
        o_diff = o_dec.reshape(NS, DW)
        hs, *proj_s = stage(hs, (o_gla.reshape(NS, GW), zg, o_diff, zd), l + 1, lam_init, NS, None)
        sk.append(kd.reshape(DB, T, HD, DK2))
        sv.append(vd.reshape(DB, T, HD, DV))
        ss.append(s_s)
    y_prompt = _final_norm(hp.reshape(B, L, D), final_norm_g[None], skip=NM, chunk=LANES)
    y_sample = _final_norm(hs.reshape(1, NS, D), final_norm_g[None], skip=0, chunk=NS).reshape(DB, T, D)
    pk, pv = (a.reshape(DEPTH, B, L, HD, a.shape[-1]) for a in kv)
    return (y_prompt, y_sample, pk, pv, jnp.stack(ps),
            jnp.stack(sk), jnp.stack(sv), jnp.stack(ss))
```

```python
import functools
import math

import jax
import jax.numpy as jnp
from jax import lax
from jax.experimental import pallas as pl
from jax.experimental.pallas import tpu as pltpu

RMS_EPS = 1e-6
NEG_INF = -1e30
GLA_GATE_NORM = 16.0
GLA_CHUNK = 64
LOG2E = 1.4426950408889634
LANES = 128
VMEM_LIMIT_BYTES = 56 * 1024 * 1024
ROW_TILE_CAP = 704
BF16 = jnp.bfloat16
F32 = jnp.float32


def _cparams(*sem):
    return pltpu.CompilerParams(dimension_semantics=sem, vmem_limit_bytes=VMEM_LIMIT_BYTES)


def _dot(a, b):
    return jnp.dot(a, b, preferred_element_type=F32)


def _dot_nt(a, b):
    return lax.dot_general(a, b, (((1,), (1,)), ((), ())), preferred_element_type=F32)


def _dot_tn(a, b):
    return lax.dot_general(a, b, (((0,), (0,)), ((), ())), preferred_element_type=F32)


def _lambda_init(layer):
    return 0.8 - 0.6 * math.exp(-0.3 * layer)


def _lam_from(lamv, lam_init):
    t1 = jnp.sum(lamv[0:1, :] * lamv[1:2, :], axis=-1, keepdims=True)
    t2 = jnp.sum(lamv[2:3, :] * lamv[3:4, :], axis=-1, keepdims=True)
    return jnp.exp(t1) - jnp.exp(t2) + lam_init


def _rms(x):
    return x * lax.rsqrt(jnp.mean(x * x, axis=-1, keepdims=True) + RMS_EPS)


def _row_stage_kernel(*refs, mix, proj, widths, gate_rank, qg_scale, qd_scale, kv_heads, kv_stacked, dv_g, dv_d,
                      diff_scale):
    refs = list(refs)
    x = refs.pop(0)[...]
    if mix:
        og_ref, zg_ref, od_ref, zd_ref, gg_ref, gd_ref, wo_ref = refs[:7]
        del refs[:7]
    if proj:
        g_ref, wt_ref, wup_ref, bg_ref = refs[:4]
        del refs[:4]
    if kv_stacked:
        del refs[:2]
    if mix:
        parts = []
        for o_ref, z_ref, gain_ref, dv, post in ((og_ref, zg_ref, gg_ref, dv_g, 1.0),
                                                  (od_ref, zd_ref, gd_ref, dv_d, diff_scale)):
            for h in range(o_ref.shape[1] // dv):
                c = slice(h * dv, (h + 1) * dv)
                z = z_ref[:, c]
                y = _rms(o_ref[:, c]) * gain_ref[...]
                if post != 1.0:
                    y = y * post
                parts.append(y * (z * jax.nn.sigmoid(z)))
        x = x + _dot(jnp.concatenate(parts, axis=-1).astype(BF16), wo_ref[...])
        refs.pop(0)[...] = x
    if proj:
        qg_ref, kg_ref, vg_ref, la_ref, zg_out, qd_ref, kd_ref, vd_ref, zd_out = refs
        u = (_rms(x) * g_ref[...]).astype(BF16)
        tm = x.shape[0]
        lo = 0
        for o_ref, wd in zip((qg_ref, kg_ref, vg_ref, zg_out, qd_ref, kd_ref, vd_ref, zd_out), widths):
            if o_ref is zg_out:
                lo += gate_rank
            y = _dot_nt(u, wt_ref[lo:lo + wd, :])
            if o_ref is qg_ref:
                y = y * qg_scale
            if o_ref is qd_ref:
                y = y * qd_scale
            if kv_heads and (o_ref is kd_ref or o_ref is vd_ref):
                for h in range(kv_heads):
                    o_ref[pl.ds(h, tm, stride=kv_heads), :] = y[:, h * LANES:(h + 1) * LANES]
            else:
                o_ref[...] = y.astype(o_ref.dtype)
            lo += wd
        gate_lo = widths[0] + widths[1] + widths[2]
        glr = _dot_nt(u, wt_ref[gate_lo:gate_lo + LANES, :]).astype(BF16)
        gate = _dot(glr, wup_ref[...]) + bg_ref[...]
        log_sig = jnp.minimum(gate, 0.0) - jnp.log1p(jnp.exp(-jnp.abs(gate)))
        la_ref[...] = log_sig / GLA_GATE_NORM


def _row_stage(h, mix_args, mix_layer, proj_args, proj_layer, kv_stack, *, widths, gate_rank, qg_scale, qd_scale, tm,
               kv_heads, dv_g, dv_d, diff_scale):
    n, d = h.shape
    row = lambda i: (i, 0)
    rows = lambda a: pl.BlockSpec((tm, a.shape[1]), row)
    layered = lambda a, layer: pl.BlockSpec((None,) + a.shape[1:], lambda i: (layer, 0, 0))
    args, in_specs, out_specs, out_shapes, aliases = [h], [rows(h)], [], [], {}
    if mix_args is not None:
        args += list(mix_args)
        in_specs += [rows(a) for a in mix_args[:4]] + [layered(a, mix_layer) for a in mix_args[4:]]
        out_specs.append(pl.BlockSpec((tm, d), row))
        out_shapes.append(jax.ShapeDtypeStruct((n, d), F32))
    if proj_args is not None:
        args += list(proj_args)
        in_specs += [layered(a, proj_layer) for a in proj_args]
        gqk, _, gw, _, dqk, _, dw, _ = widths
        for i, w in enumerate((gqk, gqk, gw, gqk, gw, dqk, dqk, dw, dw)):
            split = kv_heads if (kv_heads and i in (6, 7)) else 1
            if kv_stack is not None and i in (6, 7):
                stack = kv_stack[i - 6]
                aliases[len(args)] = len(out_specs)
                args.append(stack)
                in_specs.append(pl.BlockSpec(memory_space=pl.ANY))
                out_specs.append(pl.BlockSpec((None, tm * split, w // split), lambda r: (proj_layer, r, 0)))
                out_shapes.append(jax.ShapeDtypeStruct(stack.shape, stack.dtype))
            else:
                out_specs.append(pl.BlockSpec((tm * split, w // split), row))
                out_shapes.append(jax.ShapeDtypeStruct((n * split, w // split), BF16 if i == 5 else F32))
    return pl.pallas_call(
        functools.partial(_row_stage_kernel, mix=mix_args is not None, proj=proj_args is not None,
                          widths=widths, gate_rank=gate_rank, qg_scale=qg_scale, qd_scale=qd_scale,
                          kv_heads=kv_heads,
                          kv_stacked=bool(aliases), dv_g=dv_g, dv_d=dv_d, diff_scale=diff_scale),
        grid=(n // tm,),
        in_specs=in_specs,
        out_specs=out_specs,
        out_shape=out_shapes,
        input_output_aliases=aliases,
        compiler_params=_cparams("parallel"),
        name="row_stage",
    )(*args)


def _gla_prompt_kernel(q_ref, k_ref, v_ref, la_ref, o_ref, s_ref, st_scr, *, n_meta, n_chunks, dk, dv, pairs):
    C = GLA_CHUNK
    st_scr[...] = jnp.zeros_like(st_scr)
    tril2 = (lax.broadcasted_iota(jnp.int32, (C, 2 * C), 0)
             >= lax.rem(lax.broadcasted_iota(jnp.int32, (C, 2 * C), 1), C))
    head_a = lax.broadcasted_iota(jnp.int32, (1, 2 * dk), 1) < dk
    val_a = lax.broadcasted_iota(jnp.int32, (1, 2 * dv), 1) < dv
    block_diag = ((lax.broadcasted_iota(jnp.int32, (2 * dv, 2 * dk), 0) < dv)
                  == (lax.broadcasted_iota(jnp.int32, (2 * dv, 2 * dk), 1) < dk))
    row = lax.broadcasted_iota(jnp.int32, (C, 1), 0)

    def cumsum_rows(x):
        shift = 1
        while shift < C:
            x = x + jnp.where(row >= shift, pltpu.roll(x, shift, axis=0), 0.0)
            shift *= 2
        return x

    def chunk(r0, n_valid):
        outs = []
        for p in range(pairs):
            qk_l = slice(p * 2 * dk, (p + 1) * 2 * dk)
            v_l = slice(p * 2 * dv, (p + 1) * 2 * dv)
            la = la_ref[pl.ds(r0, C), qk_l]
            q = q_ref[pl.ds(r0, C), qk_l]
            k = k_ref[pl.ds(r0, C), qk_l]
            v = v_ref[pl.ds(r0, C), v_l]
            if n_valid < C:
                keep = row < n_valid
                la = jnp.where(keep, la, 0.0)
                k = jnp.where(keep, k, 0.0)
                v = jnp.where(keep, v, 0.0)
            b = cumsum_rows(la)
            b_last = b[C - 1:C, :]
            q_in = (q * jnp.exp(b)).astype(BF16)
            k_in = (k * jnp.exp(-b)).astype(BF16)
            k_st = (k * jnp.exp(b_last - b)).astype(BF16)
            vb = v.astype(BF16)
            zk = jnp.zeros_like(k_in)
            k_bd = jnp.concatenate([jnp.where(head_a, k_in, zk), jnp.where(head_a, zk, k_in)], axis=0)
            att = jnp.where(tril2, _dot_nt(q_in, k_bd), 0.0).astype(BF16)
            zv = jnp.zeros_like(vb)
            v_bd = jnp.concatenate([jnp.where(val_a, vb, zv), jnp.where(val_a, zv, vb)], axis=0)
            o_intra = _dot(att, v_bd)
            st = st_scr[p]
            outs.append(o_intra + _dot_nt(q_in, st.astype(BF16)))
            d_st = jnp.where(block_diag, _dot_tn(vb, k_st), 0.0)
            st_scr[p] = st * jnp.exp(b_last) + d_st
        return jnp.concatenate(outs, axis=-1)

    o0 = chunk(0, n_meta)
    o_ref[0:n_meta, :] = o0[0:n_meta, :]

    def body(c, carry):
        r0 = pl.multiple_of(n_meta + c * C, 8)
        o_ref[pl.ds(r0, C), :] = chunk(r0, C)
        return carry

    lax.fori_loop(0, n_chunks, body, 0, unroll=4)
    for p in range(pairs):
        S = st_scr[p].T
        s_ref[2 * p] = S[0:dk, 0:dv]
        s_ref[2 * p + 1] = S[dk:2 * dk, dv:2 * dv]


def _gla_prompt(qg, kg, vg, la, *, batch, seq_len, n_meta, heads, dk, dv):
    n_chunks = (seq_len - n_meta) // GLA_CHUNK
    qk_spec = pl.BlockSpec((seq_len, heads * dk), lambda b: (b, 0))
    v_spec = pl.BlockSpec((seq_len, heads * dv), lambda b: (b, 0))
    return pl.pallas_call(
        functools.partial(_gla_prompt_kernel, n_meta=n_meta, n_chunks=n_chunks, dk=dk, dv=dv, pairs=heads // 2),
        grid=(batch,),
        in_specs=[qk_spec, qk_spec, v_spec, qk_spec],
        out_specs=[v_spec, pl.BlockSpec((None, heads, dk, dv), lambda b: (b, 0, 0, 0))],
        out_shape=[jax.ShapeDtypeStruct((batch * seq_len, heads * dv), F32),
                   jax.ShapeDtypeStruct((batch, heads, dk, dv), F32)],
        scratch_shapes=[pltpu.VMEM((heads // 2, 2 * dv, 2 * dk), F32)],
        compiler_params=_cparams("parallel"),
        name="gla_prompt",
    )(qg, kg, vg, la)


def _gla_decode_kernel(qklT_ref, v_ref, s0_ref, o_ref, s_ref, *, heads, dk, dv, steps):
    hk = heads * dk
    for b in range(qklT_ref.shape[0]):
        for h in range(heads):
            S = s0_ref[b, h]
            cols = slice(h * dv, (h + 1) * dv)
            col = lambda part, t: qklT_ref[b, part * hk + h * dk:part * hk + (h + 1) * dk, t:t + 1]
            for t in range(steps):
                S = S * jnp.exp(col(2, t)) + col(1, t) * v_ref[b, t:t + 1, cols]
                o_ref[b, t:t + 1, cols] = jnp.sum(col(0, t) * S, axis=0, keepdims=True)
            s_ref[b, h] = S


def _gla_decode(qklT, v, state, layer, *, heads, dk, dv, seqs_per_step):
    db, _, steps = qklT.shape
    sb = seqs_per_step
    return pl.pallas_call(
        functools.partial(_gla_decode_kernel, heads=heads, dk=dk, dv=dv, steps=steps),
        grid=(db // sb,),
        in_specs=[pl.BlockSpec((sb,) + qklT.shape[1:], lambda i: (i, 0, 0)),
                  pl.BlockSpec((sb, steps, heads * dv), lambda i: (i, 0, 0)),
                  pl.BlockSpec((None, sb, heads, dk, dv), lambda i: (layer, i, 0, 0, 0))],
        out_specs=[pl.BlockSpec((sb, steps, heads * dv), lambda i: (i, 0, 0)),
                   pl.BlockSpec((sb, heads, dk, dv), lambda i: (i, 0, 0, 0))],
        out_shape=[jax.ShapeDtypeStruct((db, steps, heads * dv), F32),
                   jax.ShapeDtypeStruct((db, heads, dk, dv), F32)],
        compiler_params=_cparams("parallel"),
        name="gla_decode",
    )(qklT, v, state)


def _attn_prompt_kernel(slopes_ref, lamv_ref, q_ref, k_ref, v_ref, o_ref,
                        kb_scr, vb_scr, qm_scr, m_scr, acc_scr, *, heads, n_q, tq, dh, lam_init):
    seq_len = q_ref.shape[0]
    pad_len = kb_scr.shape[1]
    dv = v_ref.shape[1]
    tail = seq_len - n_q * tq
    lam = _lam_from(lamv_ref[...], lam_init)
    first = lax.broadcasted_iota(jnp.int32, (1, LANES), 1) < dh
    for h in range(heads):
        kb_scr[h, 0:seq_len, :] = k_ref[pl.ds(h, seq_len, stride=heads), :].astype(BF16)
        vb_scr[h, 0:seq_len, 0:dv] = v_ref[pl.ds(h, seq_len, stride=heads), :].astype(BF16)
        vb_scr[h, 0:seq_len, dv:2 * dv] = jnp.ones((seq_len, dv), BF16)
        kb_scr[h, seq_len:pad_len, :] = jnp.zeros((pad_len - seq_len, LANES), BF16)
        vb_scr[h, seq_len:pad_len, :] = jnp.zeros((pad_len - seq_len, 2 * dv), BF16)

    wide = 2 * tq

    def attend(tiles, n_wide):
        for t, (q0, rows, _) in enumerate(tiles):
            r2 = 2 * rows
            for h in range(heads):
                q = q_ref[pl.ds(q0, rows), h * LANES:(h + 1) * LANES]
                zero = jnp.zeros_like(q)
                qm_scr[t, h, 0:rows, :] = jnp.where(first, q, zero)
                qm_scr[t, h, rows:r2, :] = jnp.where(first, zero, q)
                m_scr[t, h, 0:r2, :] = jnp.full((r2, LANES), NEG_INF, F32)
                acc_scr[t, h, 0:r2, :] = jnp.zeros((r2, 2 * dv), F32)

        def step(t, r0, width, causal):
            q0, rows, _ = tiles[t]
            r2 = 2 * rows
            rel = (r0 + lax.broadcasted_iota(jnp.int32, (1, width), 1) - q0)
            for h in range(heads):
                s = (_dot_nt(qm_scr[t, h, 0:r2, :], kb_scr[h, pl.ds(r0, width), :])
                     + (slopes_ref[h] * LOG2E) * rel.astype(F32))
                if causal:
                    row_pos = lax.rem(lax.broadcasted_iota(jnp.int32, (r2, 1), 0), rows)
                    s = jnp.where(rel <= row_pos, s, NEG_INF)
                m_prev = m_scr[t, h, 0:r2, :]
                m_new = jnp.maximum(m_prev, jnp.max(s, axis=-1, keepdims=True))
                alpha = jnp.exp2(m_prev - m_new)
                p = jnp.exp2(s - jnp.concatenate([m_new] * (width // LANES), axis=-1)).astype(BF16)
                acc_scr[t, h, 0:r2, :] = (acc_scr[t, h, 0:r2, :] * jnp.concatenate([alpha, alpha], axis=-1)
                                          + _dot(p, vb_scr[h, pl.ds(r0, width), :]))
                m_scr[t, h, 0:r2, :] = m_new

        def wide_step(j, carry):
            for t in range(len(tiles)):
                step(t, pl.multiple_of(j * wide, wide), wide, False)
            return carry

        lax.fori_loop(0, n_wide, wide_step, 0)
        for i in range(max(len(tile[2]) for tile in tiles)):
            for t, (_, _, last_steps) in enumerate(tiles):
                if i < len(last_steps):
                    step(t, *last_steps[i])
        for t, (q0, rows, _) in enumerate(tiles):
            r2 = 2 * rows
            for h in range(heads):
                acc = acc_scr[t, h, 0:r2, :]
                o = acc[:, 0:dv] / acc[:, dv:2 * dv]
                o_ref[pl.ds(q0, rows), h * dv:(h + 1) * dv] = o[0:rows, :] - lam * o[rows:r2, :]

    def tile_pair(q0):
        return [(q0, tq, [(q0, tq, True)]), (q0 + tq, tq, [(q0, wide, True)])]

    def quad_body(i, carry):
        q0 = pl.multiple_of(i * 2 * wide, 2 * wide)
        upper = [(qa, tq, [(q0, wide, False)] + last) for qa, _, last in tile_pair(q0 + wide)]
        attend(tile_pair(q0) + upper, 2 * i)
        return carry

    lax.fori_loop(0, n_q // 4, quad_body, 0)
    done = n_q - n_q % 4
    if n_q - done >= 2:
        attend(tile_pair(done * tq), done // 2)
        done += 2
    if n_q - done:
        attend([(done * tq, tq, [(done * tq, tq, True)])], done // 2)
    if tail:
        last_r0 = (n_q - n_q % 2) * tq
        attend([(n_q * tq, tail, [(last_r0, n_q * tq + LANES - last_r0, True)])], n_q // 2)


def _attn_prompt(slopes, lamv, qd, k_stack, v_stack, layer, *, batch, seq_len, heads, dh, dv, lam_init, tq):
    n_q = seq_len // tq
    pad_len = n_q * tq + LANES if seq_len % tq else seq_len
    assert seq_len - n_q * tq <= LANES
    tok_spec = pl.BlockSpec((seq_len, heads * dv), lambda b: (b, 0))
    row_spec = pl.BlockSpec((None, seq_len * heads, dv), lambda b: (layer, b, 0))
    return pl.pallas_call(
        functools.partial(_attn_prompt_kernel, heads=heads, n_q=n_q, tq=tq, dh=dh, lam_init=lam_init),
        grid=(batch,),
        in_specs=[pl.BlockSpec(memory_space=pltpu.SMEM), pl.BlockSpec(lamv.shape, lambda b: (0, 0)),
                  tok_spec, row_spec, row_spec],
        out_specs=tok_spec,
        out_shape=jax.ShapeDtypeStruct((batch * seq_len, heads * dv), F32),
        scratch_shapes=[pltpu.VMEM((heads, pad_len, 2 * dh), BF16), pltpu.VMEM((heads, pad_len, 2 * dv), BF16),
                        pltpu.VMEM((4, heads, 2 * tq, 2 * dh), BF16),
                        pltpu.VMEM((4, heads, 2 * tq, LANES), F32), pltpu.VMEM((4, heads, 2 * tq, 2 * dv), F32)],
        compiler_params=_cparams("parallel"),
        name="attn_prompt",
    )(slopes, lamv, qd, k_stack, v_stack)


def _attn_decode_kernel(pt_ref, q_ref, base_ref, slope_ref, newbase_ref, kn_ref, vn_ref, lamv_ref, *rest,
                        pages_per_step, page, past, lam_init, heads, dh):
    del pt_ref
    k_refs = rest[:pages_per_step]
    v_refs = rest[pages_per_step:2 * pages_per_step]
    o_ref, qm_scr, kn_scr, vn_scr, m_scr, l_scr, acc_scr = rest[2 * pages_per_step:]
    j = pl.program_id(1)
    steps = q_ref.shape[0]
    half = heads * steps

    @pl.when(j == 0)
    def _():
        m_scr[...] = jnp.full_like(m_scr, NEG_INF)
        l_scr[...] = jnp.zeros_like(l_scr)
        acc_scr[...] = jnp.zeros_like(acc_scr)
        first = lax.broadcasted_iota(jnp.int32, (1, 2 * dh), 1) < dh
        for h in range(heads):
            q = q_ref[:, h * 2 * dh:(h + 1) * 2 * dh].astype(F32)
            qm_scr[h * steps:(h + 1) * steps, :] = jnp.where(first, q, 0.0)
            qm_scr[half + h * steps:half + (h + 1) * steps, :] = jnp.where(first, 0.0, q)

    qm = qm_scr[...].astype(BF16)
    slope = slope_ref[...]

    def update(scores, value_refs):
        m_prev = m_scr[...]
        tile_max = functools.reduce(jnp.maximum, scores)
        m_new = jnp.maximum(m_prev, jnp.max(tile_max, axis=-1, keepdims=True))
        alpha = jnp.exp2(m_prev - m_new)
        rep = scores[0].shape[1] // LANES
        m_rep = jnp.concatenate([m_new] * rep, axis=-1)
        p_sum = None
        pv = None
        for s, v_ref in zip(scores, value_refs):
            p = jnp.exp2(s - m_rep)
            p_sum = p if p_sum is None else p_sum + p
            d = _dot(p.astype(BF16), v_ref[...].astype(BF16))
            pv = d if pv is None else pv + d
        l_scr[...] = alpha * l_scr[...] + jnp.sum(p_sum, axis=-1, keepdims=True)
        acc_scr[...] = alpha * acc_scr[...] + pv
        m_scr[...] = m_new

    scores = []
    for i in range(pages_per_step):
        pos0 = ((j * pages_per_step + i) * page - past).astype(F32)
        scores.append(_dot_nt(qm, k_refs[i][...].astype(BF16)) + (base_ref[...] + slope * pos0))
    update(scores, v_refs)

    @pl.when(j == pl.num_programs(1) - 1)
    def _():
        n_new = kn_ref.shape[0]
        kn_scr[...] = jnp.zeros_like(kn_scr)
        vn_scr[...] = jnp.zeros_like(vn_scr)
        kn_scr[0:n_new, :] = kn_ref[...]
        vn_scr[0:n_new, :] = vn_ref[...]
        update([_dot_nt(qm, kn_scr[...].astype(BF16)) + newbase_ref[...]], [vn_scr])
        lam = _lam_from(lamv_ref[...], lam_init)
        o = acc_scr[...] / l_scr[...]
        o = o[0:half, :] - lam * o[half:2 * half, :]
        dv = vn_ref.shape[1]
        for h in range(heads):
            o_ref[:, h * dv:(h + 1) * dv] = o[h * steps:(h + 1) * steps, :]


def _attn_decode(page_table, q_tok, base, slope_rows, newbase, k_new, v_new, lamv, cache_k, cache_v, layer,
                 *, pages_per_step, page, lam_init, heads, dh):
    db, n_pages = page_table.shape
    steps = q_tok.shape[1]
    rows = 2 * heads * steps
    page_rows, width = cache_k.shape[2:]
    fixed = lambda a: pl.BlockSpec(a.shape, lambda b, j, pt: (0,) * a.ndim)
    per_seq = lambda a: pl.BlockSpec((None,) + a.shape[1:], lambda b, j, pt: (b, 0, 0))

    def page_spec(i):
        return pl.BlockSpec((None, None, page_rows, width),
                            lambda b, j, pt: (layer, pt[b, j * pages_per_step + i], 0, 0))

    pages = [page_spec(i) for i in range(pages_per_step)]
    grid_spec = pltpu.PrefetchScalarGridSpec(
        num_scalar_prefetch=1,
        grid=(db, n_pages // pages_per_step),
        in_specs=[per_seq(q_tok), fixed(base), fixed(slope_rows), fixed(newbase),
                  per_seq(k_new), per_seq(v_new), fixed(lamv)] + pages + pages,
        out_specs=pl.BlockSpec((None, steps, heads * width), lambda b, j, pt: (b, 0, 0)),
        scratch_shapes=[pltpu.VMEM((rows, 2 * dh), F32), pltpu.VMEM((LANES, 2 * dh), F32),
                        pltpu.VMEM((LANES, width), F32),
                        pltpu.VMEM((rows, LANES), F32), pltpu.VMEM((rows, LANES), F32),
                        pltpu.VMEM((rows, width), F32)],
    )
    return pl.pallas_call(
        functools.partial(_attn_decode_kernel, pages_per_step=pages_per_step, page=page,
                          past=n_pages * page, lam_init=lam_init, heads=heads, dh=dh),
        grid_spec=grid_spec,
        out_shape=jax.ShapeDtypeStruct((db, steps, heads * width), F32),
        compiler_params=_cparams("parallel", "arbitrary"),
        name="attn_decode",
    )(page_table, q_tok, base, slope_rows, newbase, k_new, v_new, lamv,
      *([cache_k] * pages_per_step), *([cache_v] * pages_per_step))


def _final_norm_kernel(h_ref, g_ref, o_ref, *, skip, chunk):
    def body(i, carry):
        x = h_ref[pl.ds(pl.multiple_of(skip + i * chunk, 8), chunk), :]
        o_ref[pl.ds(pl.multiple_of(i * chunk, 8), chunk), :] = _rms(x) * g_ref[...]
        return carry

    lax.fori_loop(0, o_ref.shape[0] // chunk, body, 0)


def _final_norm(h3, g, *, skip, chunk):
    b, l, d = h3.shape
    return pl.pallas_call(
        functools.partial(_final_norm_kernel, skip=skip, chunk=chunk),
        grid=(b,),
        in_specs=[pl.BlockSpec((None, l, d), lambda i: (i, 0, 0)), pl.BlockSpec(g.shape, lambda i: (0, 0))],
        out_specs=pl.BlockSpec((None, l - skip, d), lambda i: (i, 0, 0)),
        out_shape=jax.ShapeDtypeStruct((b, l - skip, d), F32),
        compiler_params=_cparams("parallel"),
        name="final_norm",
    )(h3, g)


def _row_tile(n, cap):
    best = 8
    for t in range(8, cap + 1, 8):
        if n % t == 0:
            best = t
    return best


def kernel(x_prompt, x_sample, cache_k, cache_v, state_gla, page_table, meta_tokens, norm_g, w_in, gla_w_up, gla_b_gate, gla_norm_g, lambda_q1, lambda_k1, lambda_q2, lambda_k2, diff_norm_g, w_out, final_norm_g):
    B, SEQ, D = x_prompt.shape
    DB, T, _ = x_sample.shape
    DEPTH, NPOOL, PAGE, HD, DK2 = cache_k.shape
    DV = cache_v.shape[-1]
    DH = DK2 // 2
    NM = meta_tokens.shape[0]
    HG, GDK, GDV = state_gla.shape[2:]
    RANK = gla_w_up.shape[1]
    L = NM + SEQ
    NP, NS = B * L, DB * T
    GQK, GW, DQK, DW = HG * GDK, HG * GDV, HD * DK2, HD * DV
    widths = (GQK, GQK, GW, GW, DQK, DQK, DW, DW)
    assert w_in.shape[-1] == sum(widths) + RANK and DK2 == LANES and DV == LANES and 2 * GDK == LANES

    w_in_t = jnp.swapaxes(w_in, 1, 2).astype(BF16)
    w_up = jnp.pad(gla_w_up, ((0, 0), (0, LANES - RANK), (0, 0))).astype(BF16)
    w_out_b = w_out.astype(BF16)
    proj_stacks = (norm_g[:, None, :], w_in_t, w_up, gla_b_gate[:, None, :])
    gain_stacks = (gla_norm_g[:, None, :], diff_norm_g[:, None, :], w_out_b)
    lamv = jnp.stack([lambda_q1, lambda_k1, lambda_q2, lambda_k2], axis=1)
    slopes = 2.0 ** (-8.0 * jnp.arange(1, HD + 1, dtype=F32) / HD)
    cache_k2 = cache_k.reshape(DEPTH, NPOOL, PAGE * HD, DK2)
    cache_v2 = cache_v.reshape(DEPTH, NPOOL, PAGE * HD, DV)

    rows = 2 * HD * T
    row_h = (jnp.arange(rows) // T) % HD
    row_t = jnp.arange(rows) % T
    row_slope = (slopes[row_h] * LOG2E)[:, None]
    slope_rows = jnp.broadcast_to(row_slope, (rows, PAGE * HD))
    col = jnp.arange(PAGE * HD)
    base = jnp.where((col % HD)[None, :] == row_h[:, None], row_slope * (col // HD)[None, :].astype(F32), NEG_INF)
    ncol = jnp.arange(LANES)
    new_ok = (ncol[None, :] < T * HD) & ((ncol % HD)[None, :] == row_h[:, None]) & ((ncol // HD)[None, :] <= row_t[:, None])
    newbase = jnp.where(new_ok, row_slope * (ncol // HD)[None, :].astype(F32), NEG_INF)

    hp = jnp.concatenate([jnp.broadcast_to(meta_tokens[None], (B, NM, D)), x_prompt], axis=1).reshape(NP, D)
    hs = x_sample.reshape(NS, D)
    tm_p = _row_tile(NP, ROW_TILE_CAP)
    n_pages = page_table.shape[1]
    pages_per_step = max(p for p in range(1, 33) if n_pages % p == 0)
    ps, sk, sv, ss = [], [], [], []

    def stage(h, mixed, l_next, lam_init, tm, kv_stack):
        opening = l_next < DEPTH
        return _row_stage(h, mixed + gain_stacks if mixed else None, l_next - 1,
                          proj_stacks if opening else None, l_next, kv_stack if opening else None,
                          widths=widths, gate_rank=RANK, qg_scale=GDK ** -0.5, qd_scale=DH ** -0.5 * LOG2E, tm=tm,
                          kv_heads=HD if kv_stack else 0, dv_g=GDV, dv_d=DV, diff_scale=1.0 - lam_init)

    kv = (jnp.zeros((DEPTH, NP * HD, DK2), F32), jnp.zeros((DEPTH, NP * HD, DV), F32))
    proj_p = stage(hp, None, 0, 0.0, tm_p, kv)
    proj_s = stage(hs, None, 0, 0.0, NS, None)
    for l in range(DEPTH):
        lam_init = _lambda_init(l)
        qg, kg, vg, la, zg, qd, k_stack, v_stack, zd = proj_p
        kv = (k_stack, v_stack)
        o_gla, s_p = _gla_prompt(qg, kg, vg, la, batch=B, seq_len=L, n_meta=NM, heads=HG, dk=GDK, dv=GDV)
        o_diff = _attn_prompt(slopes, lamv[l], qd, k_stack, v_stack, l, batch=B, seq_len=L, heads=HD,
                              dh=DH, dv=DV, lam_init=lam_init, tq=2 * LANES)
        hp, *proj_p = stage(hp, (o_gla, zg, o_diff, zd), l + 1, lam_init, tm_p, kv)
        ps.append(s_p)
        qg, kg, vg, la, zg, qd, kd, vd, zd = proj_s
        qklT = jnp.concatenate([qg, kg, la], axis=-1).reshape(DB, T, 3 * GQK).transpose(0, 2, 1)
        o_gla, s_s = _gla_decode(qklT, vg.reshape(DB, T, GW), state_gla, l, heads=HG, dk=GDK, dv=GDV,
                                 seqs_per_step=max(s for s in range(1, 9) if DB % s == 0))
        o_dec = _attn_decode(page_table, qd.reshape(DB, T, DQK), base, slope_rows, newbase,
                             kd.reshape(DB, T * HD, DK2), vd.reshape(DB, T * HD, DV), lamv[l],
                             cache_k2, cache_v2, l, pages_per_step=pages_per_step, page=PAGE, lam_init=lam_init,
                             heads=HD, dh=DH)
        o_diff = o_dec.reshape(NS, DW)
        hs, *proj_s = stage(hs, (o_gla.reshape(NS, GW), zg, o_diff, zd), l + 1, lam_init, NS, None)
        sk.append(kd.reshape(DB, T, HD, DK2))
        sv.append(vd.reshape(DB, T, HD, DV))
        ss.append(s_s)
    y_prompt = _final_norm(hp.reshape(B, L, D), final_norm_g[None], skip=NM, chunk=LANES)
    y_sample = _final_norm(hs.reshape(1, NS, D), final_norm_g[None], skip=0, chunk=NS).reshape(DB, T, D)
    pk, pv = (a.reshape(DEPTH, B, L, HD, a.shape[-1]) for a in kv)
    return (y_prompt, y_sample, pk, pv, jnp.stack(ps),
            jnp.stack(sk), jnp.stack(sv), jnp.stack(ss))
```
